```python
import functools
import jax, jax.numpy as jnp
from jax import lax
import numpy as np

D_MODEL = 1024
BATCH = 16
SEQ = 2048
DEPTH = 2

GRID_W = 64
CTX_LEN = 256
HEAD_DIM = 64
A_Q_HEADS = 8
A_KV_HEADS = 2
B_Q_HEADS = 8
B_KV_HEADS = 2
C_HEADS = 16
Q_BLOCK = 128
WINDOW = 128
NA_ROWS = 8
NA_COLS = 16
ROPE_THETA = 10000.0
D_FF = 2816
N_EXPERTS = 8
TOP_K = 2
D_FF_EXPERT = 3584
N_MOD = 6
EPS = 1e-6
NEG_INF = -1e30
ATTN_SCALE = HEAD_DIM ** -0.5

kernel_name = 'hybrid_dit_prefix_block'


def rms_norm(x, g):
    xf = x.astype(jnp.float32)
    y = xf * lax.rsqrt(jnp.mean(xf * xf, axis=-1, keepdims=True) + EPS)
    return (y * g.astype(jnp.float32)).astype(x.dtype)


def modulate(x, g, shift, scale):
    return rms_norm(x, g) * (1 + scale) + shift


def heads(t, n):
    return t.reshape(t.shape[:-1] + (n, HEAD_DIM))


def group(q, n_kv):
    b, n, hq, dh = q.shape
    return q.reshape(b, n, n_kv, hq // n_kv, dh)


def axial_rope(n_tokens):
    t = jnp.arange(n_tokens, dtype=jnp.int32)
    row = (t // GRID_W).astype(jnp.float32)
    col = (t % GRID_W).astype(jnp.float32)
    half = HEAD_DIM // 2
    freqs = ROPE_THETA ** (-jnp.arange(0, half, 2, dtype=jnp.float32) / half)
    ang = jnp.concatenate([row[:, None] * freqs, col[:, None] * freqs], axis=-1)
    return jnp.cos(ang), jnp.sin(ang)


def apply_rope(x, cos, sin):
    xf = x.astype(jnp.float32).reshape(x.shape[:-1] + (HEAD_DIM // 2, 2))
    x1, x2 = xf[..., 0], xf[..., 1]
    c, s = cos[:, None, :], sin[:, None, :]
    out = jnp.stack([x1 * c - x2 * s, x1 * s + x2 * c], axis=-1)
    return out.reshape(x.shape).astype(x.dtype)


def softmax_with_sink(s, sink):
    m = jnp.maximum(jnp.max(s, axis=-1, keepdims=True), sink)
    e = jnp.exp(s - m)
    return e / (jnp.sum(e, axis=-1, keepdims=True) + jnp.exp(sink - m))


def context_attend(q, k, v, sink):
    b, n, hkv, g, dh = q.shape
    s = jnp.einsum('bqhgd,bkhd->bhgqk', q * ATTN_SCALE, k).astype(jnp.float32)
    p = jax.nn.softmax(s, axis=-1) if sink is None else softmax_with_sink(s, sink)
    o = jnp.einsum('bhgqk,bkhd->bqhgd', p.astype(v.dtype), v)
    return o.reshape(b, n, hkv * g * dh)


def global_gqa_latent(q, k_all, v_all):
    b, s, hkv, g, dh = q.shape
    nb = s // Q_BLOCK
    qb = jnp.moveaxis(q.reshape(b, nb, Q_BLOCK, hkv, g, dh), 1, 0) * ATTN_SCALE

    def one_block(q_blk):
        sc = jnp.einsum('bqhgd,bkhd->bhgqk', q_blk, k_all).astype(jnp.float32)
        p = jax.nn.softmax(sc, axis=-1).astype(v_all.dtype)
        return jnp.einsum('bhgqk,bkhd->bqhgd', p, v_all)

    o = lax.map(one_block, qb)
    return jnp.moveaxis(o, 0, 1).reshape(b, s, hkv * g * dh)


def window_gqa_latent(q, k, v, k_ctx, v_ctx, sink):
    b, s, hkv, g, dh = q.shape
    nb = s // Q_BLOCK
    nw = 3 * Q_BLOCK

    def band(t):
        tp = jnp.pad(t, ((0, 0), (Q_BLOCK, Q_BLOCK), (0, 0), (0, 0))).reshape(b, nb + 2, Q_BLOCK, hkv, dh)
        tw = jnp.concatenate([tp[:, :-2], tp[:, 1:-1], tp[:, 2:]], axis=2)
        return jnp.moveaxis(tw, 1, 0)

    kw, vw = band(k), band(v)
    qb = jnp.moveaxis(q.reshape(b, nb, Q_BLOCK, hkv, g, dh), 1, 0) * ATTN_SCALE
    blk = jnp.arange(nb, dtype=jnp.int32)[:, None, None]
    qi = jnp.arange(Q_BLOCK, dtype=jnp.int32)[None, :, None]
    kj = jnp.arange(nw, dtype=jnp.int32)[None, None, :]
    key_pos = (blk - 1) * Q_BLOCK + kj
    allowed = (jnp.abs(kj - Q_BLOCK - qi) <= WINDOW) & (key_pos >= 0) & (key_pos < s)

    def one_block(args):
        q_blk, k_blk, v_blk, ok = args
        s_win = jnp.einsum('bqhgd,bkhd->bhgqk', q_blk, k_blk).astype(jnp.float32)
        s_win = jnp.where(ok[None, None, None], s_win, NEG_INF)
        s_ctx = jnp.einsum('bqhgd,bchd->bhgqc', q_blk, k_ctx).astype(jnp.float32)
        p = softmax_with_sink(jnp.concatenate([s_win, s_ctx], axis=-1), sink).astype(v.dtype)
        return (jnp.einsum('bhgqk,bkhd->bqhgd', p[..., :nw], v_blk)
                + jnp.einsum('bhgqc,bchd->bqhgd', p[..., nw:], v_ctx))

    o = lax.map(one_block, (qb, kw, vw, allowed))
    return jnp.moveaxis(o, 0, 1).reshape(b, s, hkv * g * dh)


def neighbourhood_latent(q, k, v, k_ctx, v_ctx, rpb, rows):
    b, s, h, dh = q.shape
    kh = min(NA_ROWS, rows)
    kg = k.reshape(b, rows, GRID_W, h, dh)
    vg = v.reshape(b, rows, GRID_W, h, dh)
    qg = jnp.moveaxis(q.reshape(b, rows, GRID_W, h, dh), 1, 0) * ATTN_SCALE
    r_idx = jnp.arange(rows, dtype=jnp.int32)
    row_start = jnp.clip(r_idx - kh // 2, 0, rows - kh)
    cols = jnp.arange(GRID_W, dtype=jnp.int32)
    col_start = jnp.clip(cols - NA_COLS // 2, 0, GRID_W - NA_COLS)
    col_in = (cols[None, :] >= col_start[:, None]) & (cols[None, :] < col_start[:, None] + NA_COLS)
    col_bias_idx = jnp.clip(cols[None, :] - cols[:, None] + NA_COLS - 1, 0, 2 * NA_COLS - 2)
    rpb32 = rpb.astype(jnp.float32)
    n_loc = kh * GRID_W

    def one_row(args):
        q_row, r, rs = args
        k_band = lax.dynamic_slice_in_dim(kg, rs, kh, axis=1)
        v_band = lax.dynamic_slice_in_dim(vg, rs, kh, axis=1)
        s_loc = jnp.einsum('bqhd,brkhd->bhqrk', q_row, k_band).astype(jnp.float32)
        row_bias_idx = rs + jnp.arange(kh, dtype=jnp.int32) - r + NA_ROWS - 1
        bias = rpb32[:, row_bias_idx[None, :, None], col_bias_idx[:, None, :]]
        s_loc = jnp.where(col_in[:, None, :], s_loc + bias[None], NEG_INF).reshape(b, h, GRID_W, n_loc)
        s_ctx = jnp.einsum('bqhd,bchd->bhqc', q_row, k_ctx).astype(jnp.float32)
        p = jax.nn.softmax(jnp.concatenate([s_loc, s_ctx], axis=-1), axis=-1).astype(v.dtype)
        return (jnp.einsum('bhqn,bnhd->bqhd', p[..., :n_loc], v_band.reshape(b, n_loc, h, dh))
                + jnp.einsum('bhqc,bchd->bqhd', p[..., n_loc:], v_ctx))

    o = lax.map(one_row, (qg, r_idx, row_start))
    return jnp.moveaxis(o, 0, 1).reshape(b, s, h * dh)


def even_mixer(h_lat, h_ctx, last, *, w_in, w_out, g_q, g_k, sink, cos, sin):
    sizes = [A_Q_HEADS * HEAD_DIM, A_KV_HEADS * HEAD_DIM, A_KV_HEADS * HEAD_DIM,
             B_Q_HEADS * HEAD_DIM, B_KV_HEADS * HEAD_DIM, B_KV_HEADS * HEAD_DIM]
    splits = np.cumsum(sizes)[:-1].tolist()

    def project(h):
        qa, ka, va, qb, kb, vb = jnp.split(h @ w_in, splits, axis=-1)
        return (rms_norm(heads(qa, A_Q_HEADS), g_q), rms_norm(heads(ka, A_KV_HEADS), g_k), heads(va, A_KV_HEADS),
                heads(qb, B_Q_HEADS), heads(kb, B_KV_HEADS), heads(vb, B_KV_HEADS))

    qa, ka, va, qb, kb, vb = project(h_lat)
    cqa, cka, cva, cqb, ckb, cvb = project(h_ctx)
    qa, ka, qb, kb = (apply_rope(t, cos, sin) for t in (qa, ka, qb, kb))
    sink4 = sink.astype(jnp.float32).reshape(B_KV_HEADS, B_Q_HEADS // B_KV_HEADS, 1, 1)
    o_a = global_gqa_latent(group(qa, A_KV_HEADS), jnp.concatenate([ka, cka], axis=1),
                            jnp.concatenate([va, cva], axis=1))
    o_b = window_gqa_latent(group(qb, B_KV_HEADS), kb, vb, ckb, cvb, sink4)
    y_lat = jnp.concatenate([o_a, o_b], axis=-1) @ w_out
    if last:
        return y_lat, None
    c_a = context_attend(group(cqa, A_KV_HEADS), cka, cva, None)
    c_b = context_attend(group(cqb, B_KV_HEADS), ckb, cvb, sink4)
    return y_lat, jnp.concatenate([c_a, c_b], axis=-1) @ w_out


def odd_mixer(h_lat, h_ctx, last, *, w_in, w_out, rpb, rows):
    def project(h):
        q, k, v = jnp.split(h @ w_in, 3, axis=-1)
        return heads(q, C_HEADS), heads(k, C_HEADS), heads(v, C_HEADS)

    q, k, v = project(h_lat)
    cq, ck, cv = project(h_ctx)
    y_lat = neighbourhood_latent(q, k, v, ck, cv, rpb, rows) @ w_out
    if last:
        return y_lat, None
    return y_lat, context_attend(cq[:, :, :, None, :], ck, cv, None) @ w_out


def swiglu(h, w_gate, w_up, w_down):
    return (jax.nn.silu(h @ w_gate) * (h @ w_up)) @ w_down


def moe_swiglu(h, w_router, w_gate, w_up, w_down):
    logits = (h @ w_router).astype(jnp.float32)
    top_val, top_idx = lax.top_k(logits, TOP_K)
    weights = jax.nn.softmax(top_val, axis=-1)
    gates = jnp.sum(jax.nn.one_hot(top_idx, N_EXPERTS, dtype=jnp.float32) * weights[..., None], axis=-2)
    gates = gates.astype(h.dtype)
    out = jnp.zeros_like(h)
    for e in range(N_EXPERTS):
        out = out + gates[..., e:e + 1] * swiglu(h, w_gate[e], w_up[e], w_down[e])
    return out


def sandwich_layer(x, ctx, c, c_ctx, mixer, ffn, last, w_mod, b_mod, g_pre_mix, g_post_mix, g_pre_ffn, g_post_ffn):
    m_lat = jnp.split(jax.nn.silu(c)[:, None, :] @ w_mod + b_mod, N_MOD, axis=-1)
    m_ctx = jnp.split(jax.nn.silu(c_ctx) @ w_mod + b_mod, N_MOD, axis=-1)
    y_lat, y_ctx = mixer(modulate(x, g_pre_mix, m_lat[0], m_lat[1]),
                         modulate(ctx, g_pre_mix, m_ctx[0], m_ctx[1]), last)
    x = x + m_lat[2] * rms_norm(y_lat, g_post_mix)
    x = x + m_lat[5] * rms_norm(ffn(modulate(x, g_pre_ffn, m_lat[3], m_lat[4])), g_post_ffn)
    if not last:
        ctx = ctx + m_ctx[2] * rms_norm(y_ctx, g_post_mix)
        ctx = ctx + m_ctx[5] * rms_norm(ffn(modulate(ctx, g_pre_ffn, m_ctx[3], m_ctx[4])), g_post_ffn)
    return x, ctx


def setup_inputs(seed: int = 0) -> dict:
    key = jax.random.key(seed)
    keys = iter(jax.random.split(key, 64))
    ne, no = (DEPTH + 1) // 2, DEPTH // 2
    in_even = (A_Q_HEADS + 2 * A_KV_HEADS + B_Q_HEADS + 2 * B_KV_HEADS) * HEAD_DIM
    mix_even = (A_Q_HEADS + B_Q_HEADS) * HEAD_DIM
    mix_odd = C_HEADS * HEAD_DIM

    def normal(shape, scale):
        return scale * jax.random.normal(next(keys), shape, jnp.float32)

    def gain(shape):
        return 1.0 + normal(shape, 0.05)

    return {
        'x': normal((BATCH, SEQ, D_MODEL), 1.0),
        'c': normal((BATCH, D_MODEL), 1.0),
        'ctx': normal((BATCH, CTX_LEN, D_MODEL), 1.0),
        'c_ctx': normal((D_MODEL,), 1.0),
        'e_w_mod': normal((ne, D_MODEL, N_MOD * D_MODEL), D_MODEL ** -0.5),
        'e_b_mod': normal((ne, N_MOD * D_MODEL), 0.02),
        'e_g_pre_mix': gain((ne, D_MODEL)),
        'e_g_post_mix': gain((ne, D_MODEL)),
        'e_g_pre_ffn': gain((ne, D_MODEL)),
        'e_g_post_ffn': gain((ne, D_MODEL)),
        'e_w_in': normal((ne, D_MODEL, in_even), D_MODEL ** -0.5),
        'e_w_out': normal((ne, mix_even, D_MODEL), mix_even ** -0.5),
        'e_g_q': gain((ne, HEAD_DIM)),
        'e_g_k': gain((ne, HEAD_DIM)),
        'e_sink': normal((ne, B_Q_HEADS), 0.5),
        'e_w_gate': normal((ne, D_MODEL, D_FF), D_MODEL ** -0.5),
        'e_w_up': normal((ne, D_MODEL, D_FF), D_MODEL ** -0.5),
        'e_w_down': normal((ne, D_FF, D_MODEL), D_FF ** -0.5),
        'o_w_mod': normal((no, D_MODEL, N_MOD * D_MODEL), D_MODEL ** -0.5),
        'o_b_mod': normal((no, N_MOD * D_MODEL), 0.02),
        'o_g_pre_mix': gain((no, D_MODEL)),
        'o_g_post_mix': gain((no, D_MODEL)),
        'o_g_pre_ffn': gain((no, D_MODEL)),
        'o_g_post_ffn': gain((no, D_MODEL)),
        'o_w_in': normal((no, D_MODEL, 3 * mix_odd), D_MODEL ** -0.5),
        'o_w_out': normal((no, mix_odd, D_MODEL), mix_odd ** -0.5),
        'o_rpb': normal((no, C_HEADS, 2 * NA_ROWS - 1, 2 * NA_COLS - 1), 0.1),
        'o_w_router': normal((no, D_MODEL, N_EXPERTS), D_MODEL ** -0.5),
        'o_w_gate': normal((no, N_EXPERTS, D_MODEL, D_FF_EXPERT), D_MODEL ** -0.5),
        'o_w_up': normal((no, N_EXPERTS, D_MODEL, D_FF_EXPERT), D_MODEL ** -0.5),
        'o_w_down': normal((no, N_EXPERTS, D_FF_EXPERT, D_MODEL), D_FF_EXPERT ** -0.5),
    }


def reference(x, c, ctx, c_ctx,
              e_w_mod, e_b_mod, e_g_pre_mix, e_g_post_mix, e_g_pre_ffn, e_g_post_ffn,
              e_w_in, e_w_out, e_g_q, e_g_k, e_sink, e_w_gate, e_w_up, e_w_down,
              o_w_mod, o_b_mod, o_g_pre_mix, o_g_post_mix, o_g_pre_ffn, o_g_post_ffn,
              o_w_in, o_w_out, o_rpb, o_w_router, o_w_gate, o_w_up, o_w_down):
    rows = x.shape[1] // GRID_W
    cos, sin = axial_rope(x.shape[1])
    for i in range(DEPTH):
        j = i // 2
        last = i == DEPTH - 1
        if i % 2 == 0:
            mixer = functools.partial(even_mixer, w_in=e_w_in[j], w_out=e_w_out[j], g_q=e_g_q[j], g_k=e_g_k[j],
                                      sink=e_sink[j], cos=cos, sin=sin)
            ffn = functools.partial(swiglu, w_gate=e_w_gate[j], w_up=e_w_up[j], w_down=e_w_down[j])
            x, ctx = sandwich_layer(x, ctx, c, c_ctx, mixer, ffn, last, e_w_mod[j], e_b_mod[j],
                                    e_g_pre_mix[j], e_g_post_mix[j], e_g_pre_ffn[j], e_g_post_ffn[j])
        else:
            mixer = functools.partial(odd_mixer, w_in=o_w_in[j], w_out=o_w_out[j], rpb=o_rpb[j], rows=rows)
            ffn = functools.partial(moe_swiglu, w_router=o_w_router[j], w_gate=o_w_gate[j], w_up=o_w_up[j],
                                    w_down=o_w_down[j])
            x, ctx = sandwich_layer(x, ctx, c, c_ctx, mixer, ffn, last, o_w_mod[j], o_b_mod[j],
                                    o_g_pre_mix[j], o_g_post_mix[j], o_g_pre_ffn[j], o_g_post_ffn[j])
    return x
```

```python
import functools

import numpy as np
import jax
import jax.numpy as jnp
from jax import lax
from jax.experimental import pallas as pl
from jax.experimental.pallas import tpu as pltpu

GRID_W = 64
HEAD_DIM = 64
A_Q_HEADS = 8
A_KV_HEADS = 2
B_Q_HEADS = 8
B_KV_HEADS = 2
C_HEADS = 16
Q_BLOCK = 128
WINDOW = 128
NA_ROWS = 8
NA_COLS = 16
ROPE_THETA = 10000.0
N_EXPERTS = 8
N_MOD = 6
EPS = 1e-6
NEG_INF = -1e30
ATTN_SCALE = HEAD_DIM ** -0.5

LANES = 128
VMEM_LIMIT = 56 * 1024 * 1024

F32 = jnp.float32
BF16 = jnp.bfloat16
HIGHEST = lax.Precision.HIGHEST


def _tile(n, pref):
    return pref if n % pref == 0 else n


def _params(*sem):
    return pltpu.CompilerParams(dimension_semantics=sem, vmem_limit_bytes=VMEM_LIMIT)


def _rms(x):
    return x * lax.rsqrt(jnp.mean(x * x, axis=-1, keepdims=True) + EPS)


def _nt_dot(a, b):
    return lax.dot_general(a, b, (((1,), (1,)), ((), ())), preferred_element_type=F32)


def _dot(a, b):
    return jnp.dot(a, b, preferred_element_type=F32)


def _mod_kernel(c_ref, w_ref, b_ref, o_ref):
    c = c_ref[...]
    a = c / (1.0 + jnp.exp(-c))
    o_ref[...] = jnp.dot(a, w_ref[...], precision=HIGHEST, preferred_element_type=F32) + b_ref[...]


def _mod_vectors(c, c_ctx, w_mod, b_mod):
    b, d = c.shape
    rows = -(-(b + 1) // 8) * 8
    cc = jnp.zeros((rows, d), F32).at[:b].set(c).at[b].set(c_ctx)
    n = w_mod.shape[1]
    tn = _tile(n, 1536)
    m = pl.pallas_call(
        _mod_kernel,
        out_shape=jax.ShapeDtypeStruct((rows, n), F32),
        grid=(n // tn,),
        in_specs=[pl.BlockSpec((rows, d), lambda j: (0, 0)),
                  pl.BlockSpec((d, tn), lambda j: (0, j)),
                  pl.BlockSpec((1, tn), lambda j: (0, j))],
        out_specs=pl.BlockSpec((rows, tn), lambda j: (0, j)),
        compiler_params=_params("arbitrary"),
        name="mod_vectors",
    )(cc, w_mod, b_mod.reshape(1, n))
    return m.reshape(rows, N_MOD, d)


def _modulated(x_ref, m_ref, g_ref):
    x = x_ref[0]
    return (_rms(x) * g_ref[...]) * (1.0 + m_ref[0, 1:2, :]) + m_ref[0, 0:1, :]


def _rope_block(y, cos, sin_signed):
    lane = lax.broadcasted_iota(jnp.int32, y.shape, 1)
    partner = jnp.where(lane % HEAD_DIM < HEAD_DIM // 2,
                        pltpu.roll(y, LANES - HEAD_DIM // 2, 1),
                        pltpu.roll(y, HEAD_DIM // 2, 1))
    return y * cos + partner * sin_signed


def _inproj_even_kernel(x_ref, m_ref, g_ref, w_ref, gq_ref, gk_ref, hm_ref, cos_ref, sin_ref, o_ref, *, rope):
    h = _modulated(x_ref, m_ref, g_ref).astype(BF16)
    y = _dot(h, w_ref[...])
    nq = A_Q_HEADS * HEAD_DIM
    n_qblk = 2 * nq // LANES
    for i in range(y.shape[1] // LANES):
        yb = y[:, i * LANES:(i + 1) * LANES]
        is_qa = i < nq // LANES
        is_q = i < n_qblk
        is_ka = i == n_qblk
        is_kb = i == n_qblk + 2
        if is_qa or is_ka:
            ms = _dot((yb * yb).astype(BF16), hm_ref[...])
            yb = yb * lax.rsqrt(ms + EPS) * (gq_ref[...] if is_qa else gk_ref[...])
        if rope and (is_q or is_ka or is_kb):
            yb = _rope_block(yb, cos_ref[...], sin_ref[...])
        if is_q:
            yb = yb * ATTN_SCALE
        o_ref[0, :, i * LANES:(i + 1) * LANES] = yb.astype(o_ref.dtype)


def _inproj_plain_kernel(x_ref, m_ref, g_ref, w_ref, o_ref, *, n_scaled):
    h = _modulated(x_ref, m_ref, g_ref).astype(BF16)
    y = _dot(h, w_ref[...])
    if n_scaled:
        o_ref[0, :, :n_scaled] = (y[:, :n_scaled] * ATTN_SCALE).astype(o_ref.dtype)
    o_ref[0, :, n_scaled:] = y[:, n_scaled:].astype(o_ref.dtype)


def _inproj(x, mods, mod_off, g, w, *, even=None, rope_tabs=None, n_scaled=0, name):
    grp, n, d = x.shape
    nout = w.shape[1]
    tm = _tile(n, 512)
    grid = (grp, n // tm)
    in_specs = [pl.BlockSpec((1, tm, d), lambda b, t: (b, t, 0)),
                pl.BlockSpec((1, N_MOD, d), lambda b, t: (b + mod_off, 0, 0)),
                pl.BlockSpec((1, d), lambda b, t: (0, 0)),
                pl.BlockSpec((d, nout), lambda b, t: (0, 0))]
    args = [x, mods, g.reshape(1, d), w]
    if even is not None:
        gq, gk, hm = even
        cos, sin = rope_tabs
        rope = cos is not None
        if not rope:
            cos = jnp.zeros((tm, LANES), F32)
            sin = cos
            tab_map = lambda b, t: (0, 0)
        else:
            tab_map = lambda b, t: (t, 0)
        in_specs += [pl.BlockSpec((1, LANES), lambda b, t: (0, 0)),
                     pl.BlockSpec((1, LANES), lambda b, t: (0, 0)),
                     pl.BlockSpec((LANES, LANES), lambda b, t: (0, 0)),
                     pl.BlockSpec((tm, LANES), tab_map),
                     pl.BlockSpec((tm, LANES), tab_map)]
        args += [gq, gk, hm, cos, sin]
        body = functools.partial(_inproj_even_kernel, rope=rope)
    else:
        body = functools.partial(_inproj_plain_kernel, n_scaled=n_scaled)
    return pl.pallas_call(
        body,
        out_shape=jax.ShapeDtypeStruct((grp, n, nout), BF16),
        grid=grid,
        in_specs=in_specs,
        out_specs=pl.BlockSpec((1, tm, nout), lambda b, t: (b, t, 0)),
        compiler_params=_params("parallel", "parallel"),
        name=name,
    )(*args)


def _stack_pair(qp):
    lane = lax.broadcasted_iota(jnp.int32, qp.shape, 1)
    zero = jnp.zeros_like(qp)
    return jnp.concatenate([jnp.where(lane < HEAD_DIM, qp, zero),
                            jnp.where(lane >= HEAD_DIM, qp, zero)], axis=0)


def _unstack_pair(o):
    t = o.shape[0] // 2
    lane = lax.broadcasted_iota(jnp.int32, (t, LANES), 1)
    return jnp.where(lane < HEAD_DIM, o[:t], o[t:])


def _softmax_pv(scores, values, sink=None):
    m = functools.reduce(jnp.maximum, [jnp.max(s, axis=-1, keepdims=True) for s in scores])
    if sink is not None:
        m = jnp.maximum(m, sink)
    den = None
    out = None
    for s, v in zip(scores, values):
        p = jnp.exp(s - m)
        ps = jnp.sum(p, axis=-1, keepdims=True)
        pv = _dot(p.astype(v.dtype), v)
        den = ps if den is None else den + ps
        out = pv if out is None else out + pv
    if sink is not None:
        den = den + jnp.exp(sink - m)
    return out * (1.0 / den)


def _pair_sink(sink_ref, h0, h1, t):
    row = lax.broadcasted_iota(jnp.int32, (2 * t, 1), 0)
    return jnp.where(row < t, sink_ref[h0], sink_ref[h1])


def _pair_heads(p):
    return p, p + A_Q_HEADS // A_KV_HEADS


def _even_attn_kernel(sink_ref, q_ref, kv_ref, ckv_ref, o_ref, *, seq, band):
    j = pl.program_id(1)
    tq = q_ref.shape[1]
    n_pairs = A_Q_HEADS * HEAD_DIM // LANES
    cka, cva = ckv_ref[0, :, 0:LANES], ckv_ref[0, :, LANES:2 * LANES]
    ckb, cvb = ckv_ref[0, :, 2 * LANES:3 * LANES], ckv_ref[0, :, 3 * LANES:4 * LANES]
    ka, va = kv_ref[0, :, 0:LANES], kv_ref[0, :, LANES:2 * LANES]
    for p in range(n_pairs):
        qs = _stack_pair(q_ref[0, :, p * LANES:(p + 1) * LANES])
        o = _softmax_pv([_nt_dot(qs, ka), _nt_dot(qs, cka)], [va, cva])
        o_ref[0, :, p * LANES:(p + 1) * LANES] = _unstack_pair(o).astype(o_ref.dtype)
    start = pl.multiple_of(jnp.clip((j - 1) * tq, 0, seq - band), Q_BLOCK)
    kb = kv_ref[0, pl.ds(start, band), 2 * LANES:3 * LANES]
    vb = kv_ref[0, pl.ds(start, band), 3 * LANES:4 * LANES]
    qpos = j * tq + lax.broadcasted_iota(jnp.int32, (tq, band), 0)
    kpos = start + lax.broadcasted_iota(jnp.int32, (tq, band), 1)
    ok = jnp.abs(kpos - qpos) <= WINDOW
    ok = jnp.concatenate([ok, ok], axis=0)
    off = n_pairs * LANES
    for p in range(n_pairs):
        h0, h1 = _pair_heads(p)
        qs = _stack_pair(q_ref[0, :, off + p * LANES:off + (p + 1) * LANES])
        s_win = jnp.where(ok, _nt_dot(qs, kb), NEG_INF)
        o = _softmax_pv([s_win, _nt_dot(qs, ckb)], [vb, cvb], _pair_sink(sink_ref, h0, h1, tq))
        o_ref[0, :, off + p * LANES:off + (p + 1) * LANES] = _unstack_pair(o).astype(o_ref.dtype)


def _even_attention(qkv, ckv, sink):
    b, s, _ = qkv.shape
    c = ckv.shape[1]
    tq = Q_BLOCK
    nq = 2 * A_Q_HEADS * HEAD_DIM
    nkv = 4 * LANES
    band = min(3 * Q_BLOCK, s)
    return pl.pallas_call(
        functools.partial(_even_attn_kernel, seq=s, band=band),
        out_shape=jax.ShapeDtypeStruct((b, s, nq), BF16),
        grid=(b, s // tq),
        in_specs=[pl.BlockSpec(memory_space=pltpu.SMEM),
                  pl.BlockSpec((1, tq, nq), lambda i, j: (i, j, 0)),
                  pl.BlockSpec((1, s, nkv), lambda i, j: (i, 0, nq // nkv)),
                  pl.BlockSpec((1, c, nkv), lambda i, j: (i, 0, nq // nkv))],
        out_specs=pl.BlockSpec((1, tq, nq), lambda i, j: (i, j, 0)),
        compiler_params=_params("parallel", "parallel"),
        name="even_attention",
    )(sink, qkv, qkv, ckv)


def _even_ctx_attn_kernel(sink_ref, q_ref, ckv_ref, o_ref):
    tq = q_ref.shape[1]
    n_pairs = A_Q_HEADS * HEAD_DIM // LANES
    cka, cva = ckv_ref[0, :, 0:LANES], ckv_ref[0, :, LANES:2 * LANES]
    ckb, cvb = ckv_ref[0, :, 2 * LANES:3 * LANES], ckv_ref[0, :, 3 * LANES:4 * LANES]
    off = n_pairs * LANES
    for p in range(n_pairs):
        qs = _stack_pair(q_ref[0, :, p * LANES:(p + 1) * LANES])
        o = _softmax_pv([_nt_dot(qs, cka)], [cva])
        o_ref[0, :, p * LANES:(p + 1) * LANES] = _unstack_pair(o).astype(o_ref.dtype)
        h0, h1 = _pair_heads(p)
        qs = _stack_pair(q_ref[0, :, off + p * LANES:off + (p + 1) * LANES])
        o = _softmax_pv([_nt_dot(qs, ckb)], [cvb], _pair_sink(sink_ref, h0, h1, tq))
        o_ref[0, :, off + p * LANES:off + (p + 1) * LANES] = _unstack_pair(o).astype(o_ref.dtype)


def _even_ctx_attention(ckv, sink):
    b, c, _ = ckv.shape
    nq = 2 * A_Q_HEADS * HEAD_DIM
    nkv = 4 * LANES
    return pl.pallas_call(
        _even_ctx_attn_kernel,
        out_shape=jax.ShapeDtypeStruct((b, c, nq), BF16),
        grid=(b,),
        in_specs=[pl.BlockSpec(memory_space=pltpu.SMEM),
                  pl.BlockSpec((1, c, nq), lambda i: (i, 0, 0)),
                  pl.BlockSpec((1, c, nkv), lambda i: (i, 0, nq // nkv))],
        out_specs=pl.BlockSpec((1, c, nq), lambda i: (i, 0, 0)),
        compiler_params=_params("parallel"),
        name="even_ctx_attention",
    )(sink, ckv, ckv)


def _na_kernel(q_ref, k_ref, v_ref, ck_ref, cv_ref, bias_ref, o_ref, *, rows, kh):
    r = pl.program_id(1)
    rs = jnp.clip(r - kh // 2, 0, rows - kh)
    start = pl.multiple_of(rs * GRID_W, GRID_W)
    n_loc = kh * GRID_W
    for p in range(C_HEADS * HEAD_DIM // LANES):
        sl = slice(p * LANES, (p + 1) * LANES)
        qs = _stack_pair(q_ref[0, :, sl])
        kb = k_ref[0, pl.ds(start, n_loc), sl]
        vb = v_ref[0, pl.ds(start, n_loc), sl]
        s_loc = _nt_dot(qs, kb) + bias_ref[0, p]
        o = _softmax_pv([s_loc, _nt_dot(qs, ck_ref[0, :, sl])], [vb, cv_ref[0, :, sl]])
        o_ref[0, :, sl] = _unstack_pair(o).astype(o_ref.dtype)


def _na_bias_table(rpb, rows):
    kh = min(NA_ROWS, rows)
    cols = np.arange(GRID_W)
    col_start = np.clip(cols - NA_COLS // 2, 0, GRID_W - NA_COLS)
    col_in = (cols[None, :] >= col_start[:, None]) & (cols[None, :] < col_start[:, None] + NA_COLS)
    col_idx = np.clip(cols[None, :] - cols[:, None] + NA_COLS - 1, 0, 2 * NA_COLS - 2)
    d = np.arange(kh)[:, None]
    row_idx = np.arange(kh)[None, :] - d + NA_ROWS - 1
    rpb32 = rpb.astype(F32)
    bias = rpb32[:, row_idx[:, None, :, None], col_idx[None, :, None, :]]
    bias = jnp.where(col_in[None, None, :, None, :], bias, NEG_INF)
    bias = jnp.moveaxis(bias, 1, 0).reshape(kh, C_HEADS // 2, 2 * GRID_W, kh * GRID_W)
    return bias


def _na_attention(qkv, ckv, bias, rows):
    b, s, _ = qkv.shape
    c = ckv.shape[1]
    hd = C_HEADS * HEAD_DIM
    kh = min(NA_ROWS, rows)

    def bias_map(i, r):
        return (r - jnp.clip(r - kh // 2, 0, rows - kh), 0, 0, 0)

    return pl.pallas_call(
        functools.partial(_na_kernel, rows=rows, kh=kh),
        out_shape=jax.ShapeDtypeStruct((b, s, hd), BF16),
        grid=(b, rows),
        in_specs=[pl.BlockSpec((1, GRID_W, hd), lambda i, r: (i, r, 0)),
                  pl.BlockSpec((1, s, hd), lambda i, r: (i, 0, 1)),
                  pl.BlockSpec((1, s, hd), lambda i, r: (i, 0, 2)),
                  pl.BlockSpec((1, c, hd), lambda i, r: (i, 0, 0)),
                  pl.BlockSpec((1, c, hd), lambda i, r: (i, 0, 1)),
                  pl.BlockSpec((1,) + bias.shape[1:], bias_map)],
        out_specs=pl.BlockSpec((1, GRID_W, hd), lambda i, r: (i, r, 0)),
        compiler_params=_params("parallel", "parallel"),
        name="na_attention",
    )(qkv, qkv, qkv, ckv, ckv, bias)


def _outproj_core(o_ref, w_ref, x_ref, m_ref, gpost_ref, gpre_ref):
    y = _dot(o_ref[0], w_ref[...])
    x = x_ref[0] + m_ref[0, 2:3, :] * (_rms(y) * gpost_ref[...])
    h = (_rms(x) * gpre_ref[...]) * (1.0 + m_ref[0, 4:5, :]) + m_ref[0, 3:4, :]
    return x, h


def _outproj_kernel(o_ref, w_ref, x_ref, m_ref, gpost_ref, gpre_ref, xo_ref, h_ref):
    x, h = _outproj_core(o_ref, w_ref, x_ref, m_ref, gpost_ref, gpre_ref)
    xo_ref[0] = x
    h_ref[0] = h.astype(h_ref.dtype)


def _split_bf16(a):
    hi = a.astype(BF16)
    return hi, (a - hi.astype(F32)).astype(BF16)


def _outproj_router_kernel(o_ref, w_ref, x_ref, m_ref, gpost_ref, gpre_ref, wr_hi_ref, wr_lo_ref,
                           xo_ref, h_ref, route_ref, cnt_ref, carry_ref):
    first = (pl.program_id(0) == 0) & (pl.program_id(1) == 0)

    @pl.when(first)
    def _():
        carry_ref[...] = jnp.zeros_like(carry_ref)

    x, h = _outproj_core(o_ref, w_ref, x_ref, m_ref, gpost_ref, gpre_ref)
    xo_ref[0] = x
    h_ref[0] = h
    h_hi, h_lo = _split_bf16(h)
    logits = _dot(h_hi, wr_hi_ref[...]) + (_dot(h_hi, wr_lo_ref[...]) + _dot(h_lo, wr_hi_ref[...]))
    t = logits.shape[0]
    lane = lax.broadcasted_iota(jnp.int32, logits.shape, 1).astype(F32)
    logits = jnp.where(lane < N_EXPERTS, logits, -jnp.inf)
    v1 = jnp.max(logits, axis=-1, keepdims=True)
    i1 = jnp.min(jnp.where(logits == v1, lane, float(LANES)), axis=-1, keepdims=True)
    rest = jnp.where(lane == i1, -jnp.inf, logits)
    v2 = jnp.max(rest, axis=-1, keepdims=True)
    i2 = jnp.min(jnp.where(rest == v2, lane, float(LANES)), axis=-1, keepdims=True)
    e2 = jnp.exp(v2 - v1)
    w1 = 1.0 / (1.0 + e2)
    w2 = e2 / (1.0 + e2)
    sel = ((lane == i1) | (lane == i2)).astype(F32)
    rr = lax.broadcasted_iota(jnp.int32, (t, t), 0)
    cc = lax.broadcasted_iota(jnp.int32, (t, t), 1)
    before = _dot(jnp.where(rr > cc, 1.0, 0.0).astype(BF16), sel.astype(BF16)) + carry_ref[...]
    r1 = jnp.sum(jnp.where(lane == i1, before, 0.0), axis=-1, keepdims=True)
    r2 = jnp.sum(jnp.where(lane == i2, before, 0.0), axis=-1, keepdims=True)
    carry = carry_ref[...] + jnp.sum(sel, axis=0, keepdims=True)
    carry_ref[...] = carry
    cnt_ref[...] = carry
    fields = [i1, i2, w1, w2, r1, r2]
    route = jnp.zeros((t, LANES), F32)
    for k, f in enumerate(fields):
        route = jnp.where(lane == k, f, route)
    route_ref[0] = route[:, :route_ref.shape[2]]


def _outproj(o, w, x, mods, mod_off, g_post, g_pre, *, router=None, name):
    grp, n, d = x.shape
    dm = o.shape[2]
    tm = _tile(n, 512)
    tok = lambda b, t: (b, t, 0)
    const = lambda b, t: (0, 0)
    in_specs = [pl.BlockSpec((1, tm, dm), tok),
                pl.BlockSpec((dm, d), const),
                pl.BlockSpec((1, tm, d), tok),
                pl.BlockSpec((1, N_MOD, d), lambda b, t: (b + mod_off, 0, 0)),
                pl.BlockSpec((1, d), const),
                pl.BlockSpec((1, d), const)]
    args = [o, w, x, mods, g_post.reshape(1, d), g_pre.reshape(1, d)]
    if router is None:
        return pl.pallas_call(
            _outproj_kernel,
            out_shape=(jax.ShapeDtypeStruct((grp, n, d), F32), jax.ShapeDtypeStruct((grp, n, d), BF16)),
            grid=(grp, n // tm),
            in_specs=in_specs,
            out_specs=(pl.BlockSpec((1, tm, d), tok), pl.BlockSpec((1, tm, d), tok)),
            compiler_params=_params("parallel", "parallel"),
            name=name,
        )(*args)
    wr_hi, wr_lo = router
    in_specs += [pl.BlockSpec((d, LANES), const), pl.BlockSpec((d, LANES), const)]
    args += [wr_hi, wr_lo]
    return pl.pallas_call(
        _outproj_router_kernel,
        out_shape=(jax.ShapeDtypeStruct((grp, n, d), F32), jax.ShapeDtypeStruct((grp, n, d), F32),
                   jax.ShapeDtypeStruct((grp, n, 8), F32), jax.ShapeDtypeStruct((1, LANES), F32)),
        grid=(grp, n // tm),
        in_specs=in_specs,
        out_specs=(pl.BlockSpec((1, tm, d), tok), pl.BlockSpec((1, tm, d), tok),
                   pl.BlockSpec((1, tm, 8), tok), pl.BlockSpec((1, LANES), const)),
        scratch_shapes=[pltpu.VMEM((1, LANES), F32)],
        compiler_params=_params("arbitrary", "arbitrary"),
        name=name,
    )(*args)


def _swiglu_chunk(h, wg, wu, wd):
    g = _dot(h, wg)
    u = _dot(h, wu)
    a = (g / (1.0 + jnp.exp(-g))) * u
    return _dot(a.astype(BF16), wd)


def _ffn_kernel(h_ref, wg_ref, wu_ref, wd_ref, x_ref, m_ref, g_ref, o_ref, acc_ref):
    c = pl.program_id(2)
    part = _swiglu_chunk(h_ref[0], wg_ref[...], wu_ref[...], wd_ref[...])

    @pl.when(c == 0)
    def _():
        acc_ref[...] = part

    @pl.when(c > 0)
    def _():
        acc_ref[...] += part

    @pl.when(c == pl.num_programs(2) - 1)
    def _():
        o_ref[0] = x_ref[0] + m_ref[0, 5:6, :] * (_rms(acc_ref[...]) * g_ref[...])


def _ffn(h, wg, wu, wd, x, mods, mod_off, g_post, *, name):
    grp, n, d = x.shape
    ff = wg.shape[1]
    tm = _tile(n, 512)
    fc = _tile(ff, 1408)
    tok = lambda b, t, c: (b, t, 0)
    return pl.pallas_call(
        _ffn_kernel,
        out_shape=jax.ShapeDtypeStruct((grp, n, d), F32),
        grid=(grp, n // tm, ff // fc),
        in_specs=[pl.BlockSpec((1, tm, d), tok),
                  pl.BlockSpec((d, fc), lambda b, t, c: (0, c)),
                  pl.BlockSpec((d, fc), lambda b, t, c: (0, c)),
                  pl.BlockSpec((fc, d), lambda b, t, c: (c, 0)),
                  pl.BlockSpec((1, tm, d), tok),
                  pl.BlockSpec((1, N_MOD, d), lambda b, t, c: (b + mod_off, 0, 0)),
                  pl.BlockSpec((1, d), lambda b, t, c: (0, 0))],
        out_specs=pl.BlockSpec((1, tm, d), tok),
        scratch_shapes=[pltpu.VMEM((tm, d), F32)],
        compiler_params=_params("parallel", "parallel", "arbitrary"),
        name=name,
    )(h, wg, wu, wd, x, mods, g_post.reshape(1, d))


def _dispatch_kernel(pos_ref, h_ref, xs_in_ref, xs_ref, sem):
    del xs_in_ref
    tm = h_ref.shape[0]

    def copy(t, k):
        return pltpu.make_async_copy(h_ref.at[pl.ds(t, 1)], xs_ref.at[pl.ds(pos_ref[2 * t + k], 1)], sem)

    def issue(t, carry):
        copy(t, 0).start()
        copy(t, 1).start()
        return carry

    def drain(t, carry):
        copy(t, 0).wait()
        copy(t, 1).wait()
        return carry

    lax.fori_loop(0, tm, issue, 0)
    lax.fori_loop(0, tm, drain, 0)


def _dispatch(h, pos_flat, n_rows):
    n, d = h.shape
    tm = _tile(n, 512)
    zeros = jnp.zeros((n_rows, d), h.dtype)
    return pl.pallas_call(
        _dispatch_kernel,
        out_shape=jax.ShapeDtypeStruct((n_rows, d), h.dtype),
        grid=(n // tm,),
        in_specs=[pl.BlockSpec((2 * tm,), lambda i: (i,), memory_space=pltpu.SMEM),
                  pl.BlockSpec((tm, d), lambda i: (i, 0)),
                  pl.BlockSpec(memory_space=pl.ANY)],
        out_specs=pl.BlockSpec(memory_space=pl.ANY),
        scratch_shapes=[pltpu.SemaphoreType.DMA],
        input_output_aliases={2: 0},
        compiler_params=_params("arbitrary"),
        name="moe_dispatch",
    )(pos_flat, h, zeros)


def _experts_kernel(te_ref, nt_ref, xs_ref, wg_ref, wu_ref, wd_ref, ys_ref, xb_ref):
    i = pl.program_id(0)
    c = pl.program_id(1)

    @pl.when(i < nt_ref[0])
    def _():
        @pl.when(c == 0)
        def _():
            xb_ref[...] = xs_ref[...].astype(BF16)

        part = _swiglu_chunk(xb_ref[...], wg_ref[0], wu_ref[0], wd_ref[0])

        @pl.when(c == 0)
        def _():
            ys_ref[...] = part

        @pl.when(c > 0)
        def _():
            ys_ref[...] += part

    @pl.when((i >= nt_ref[0]) & (c == 0))
    def _():
        ys_ref[...] = jnp.zeros_like(ys_ref)


def _experts(xs, tile_expert, n_tiles, wg, wu, wd, tmg):
    n_rows, d = xs.shape
    ff = wg.shape[2]
    fc = _tile(ff, 896)
    total = n_rows // tmg

    def row_map(i, c, te, nt):
        return (jnp.minimum(i, nt[0] - 1), 0)

    def w_in_map(i, c, te, nt):
        return (te[jnp.minimum(i, nt[0] - 1)], 0, jnp.where(i < nt[0], c, ff // fc - 1))

    def w_out_map(i, c, te, nt):
        return (te[jnp.minimum(i, nt[0] - 1)], jnp.where(i < nt[0], c, ff // fc - 1), 0)

    return pl.pallas_call(
        _experts_kernel,
        out_shape=jax.ShapeDtypeStruct((n_rows, d), F32),
        grid_spec=pltpu.PrefetchScalarGridSpec(
            num_scalar_prefetch=2,
            grid=(total, ff // fc),
            in_specs=[pl.BlockSpec((tmg, d), row_map),
                      pl.BlockSpec((1, d, fc), w_in_map),
                      pl.BlockSpec((1, d, fc), w_in_map),
                      pl.BlockSpec((1, fc, d), w_out_map)],
            out_specs=pl.BlockSpec((tmg, d), lambda i, c, te, nt: (i, 0)),
            scratch_shapes=[pltpu.VMEM((tmg, d), BF16)]),
        compiler_params=_params("arbitrary", "arbitrary"),
        name="moe_experts",
    )(tile_expert, n_tiles, xs, wg, wu, wd)


def _combine_kernel(pos_ref, ys_ref, route_ref, x_ref, m_ref, g_ref, o_ref, buf_ref, sem):
    tm = x_ref.shape[1]

    def copy(t, k):
        return pltpu.make_async_copy(ys_ref.at[pl.ds(pos_ref[2 * t + k], 1)], buf_ref.at[k, pl.ds(t, 1)], sem)

    def issue(t, carry):
        copy(t, 0).start()
        copy(t, 1).start()
        return carry

    def drain(t, carry):
        copy(t, 0).wait()
        copy(t, 1).wait()
        return carry

    lax.fori_loop(0, tm, issue, 0)
    lax.fori_loop(0, tm, drain, 0)
    route = route_ref[0]
    y = route[:, 2:3] * buf_ref[0] + route[:, 3:4] * buf_ref[1]
    o_ref[0] = x_ref[0] + m_ref[0, 5:6, :] * (_rms(y) * g_ref[...])


def _combine(ys, pos_flat, route, x, mods, g_post):
    b, s, d = x.shape
    tm = _tile(s, 512)
    per = s // tm
    tok = lambda i, t: (i, t, 0)
    return pl.pallas_call(
        _combine_kernel,
        out_shape=jax.ShapeDtypeStruct((b, s, d), F32),
        grid=(b, per),
        in_specs=[pl.BlockSpec((2 * tm,), lambda i, t: (i * per + t,), memory_space=pltpu.SMEM),
                  pl.BlockSpec(memory_space=pl.ANY),
                  pl.BlockSpec((1, tm, 8), tok),
                  pl.BlockSpec((1, tm, d), tok),
                  pl.BlockSpec((1, N_MOD, d), lambda i, t: (i, 0, 0)),
                  pl.BlockSpec((1, d), lambda i, t: (0, 0))],
        out_specs=pl.BlockSpec((1, tm, d), tok),
        scratch_shapes=[pltpu.VMEM((2, tm, d), F32), pltpu.SemaphoreType.DMA],
        compiler_params=_params("arbitrary", "arbitrary"),
        name="moe_combine",
    )(pos_flat, ys, route, x, mods, g_post.reshape(1, d))


def _moe(h, route, counts, x, mods, g_post, wg, wu, wd):
    b, s, d = x.shape
    n = b * s
    tmg = _tile(2 * n, 1024)
    total_tiles = 2 * n // tmg + N_EXPERTS
    cnt = counts[0, :N_EXPERTS].astype(jnp.int32)
    tiles_per = (cnt + tmg - 1) // tmg
    tile_end = jnp.cumsum(tiles_per)
    offsets = (tile_end - tiles_per) * tmg
    r = route.reshape(n, 8)
    idx = r[:, 0:2].astype(jnp.int32)
    pos = offsets[idx] + r[:, 4:6].astype(jnp.int32)
    pos_flat = pos.reshape(2 * n)
    tile_expert = jnp.minimum(jnp.searchsorted(tile_end, jnp.arange(total_tiles), side="right"),
                              N_EXPERTS - 1).astype(jnp.int32)
    n_tiles = tile_end[-1:].astype(jnp.int32)
    xs = _dispatch(h.reshape(n, d), pos_flat, total_tiles * tmg)
    ys = _experts(xs, tile_expert, n_tiles, wg, wu, wd, tmg)
    return _combine(ys, pos_flat, route, x, mods, g_post)


def _even_layouts(w_in, w_out, g_q, g_k, sink):
    dh, half = HEAD_DIM, HEAD_DIM // 2
    within = np.concatenate([np.arange(0, dh, 2), np.arange(1, dh, 2)])
    g = A_Q_HEADS // A_KV_HEADS
    q_order = np.array([h for p in range(g) for h in _pair_heads(p)])
    sizes = [A_Q_HEADS * dh, A_KV_HEADS * dh, A_KV_HEADS * dh, B_Q_HEADS * dh, B_KV_HEADS * dh, B_KV_HEADS * dh]
    base = np.concatenate([[0], np.cumsum(sizes)])

    def head_cols(start, order, permute):
        inner = within if permute else np.arange(dh)
        return np.concatenate([start + h * dh + inner for h in order])

    kv_order = np.arange(A_KV_HEADS)
    cols = np.concatenate([head_cols(base[0], q_order, True), head_cols(base[3], q_order, True),
                           head_cols(base[1], kv_order, True), head_cols(base[2], kv_order, False),
                           head_cols(base[4], kv_order, True), head_cols(base[5], kv_order, False)])
    rows_out = np.concatenate([head_cols(0, q_order, False), head_cols(A_Q_HEADS * dh, q_order, False)])
    gq = jnp.tile(g_q[within], LANES // dh).reshape(1, LANES)
    gk = jnp.tile(g_k[within], LANES // dh).reshape(1, LANES)
    lane = np.arange(LANES)
    head_mean = jnp.asarray((lane[:, None] // dh == lane[None, :] // dh) / dh, BF16)
    return (w_in[:, cols].astype(BF16), w_out[rows_out].astype(BF16), gq, gk, head_mean,
            sink.astype(F32))


def _rope_tables(n_tokens):
    t = jnp.arange(n_tokens, dtype=jnp.int32)
    row = (t // GRID_W).astype(F32)
    col = (t % GRID_W).astype(F32)
    half = HEAD_DIM // 2
    freqs = ROPE_THETA ** (-jnp.arange(0, half, 2, dtype=F32) / half)
    ang = jnp.concatenate([row[:, None] * freqs, col[:, None] * freqs], axis=-1)
    cos, sin = jnp.cos(ang), jnp.sin(ang)
    reps = LANES // HEAD_DIM
    return (jnp.tile(jnp.concatenate([cos, cos], axis=-1), (1, reps)),
            jnp.tile(jnp.concatenate([-sin, sin], axis=-1), (1, reps)))


def kernel(x, c, ctx, c_ctx, e_w_mod, e_b_mod, e_g_pre_mix, e_g_post_mix, e_g_pre_ffn, e_g_post_ffn, e_w_in, e_w_out, e_g_q, e_g_k, e_sink, e_w_gate, e_w_up, e_w_down, o_w_mod, o_b_mod, o_g_pre_mix, o_g_post_mix, o_g_pre_ffn, o_g_post_ffn, o_w_in, o_w_out, o_rpb, o_w_router, o_w_gate, o_w_up, o_w_down):
    b, s, d = x.shape
    n_ctx = ctx.shape[1]
    rows = s // GRID_W
    ctx = ctx.reshape(1, b * n_ctx, d)

    mods = _mod_vectors(c, c_ctx, e_w_mod[0], e_b_mod[0])
    w_in, w_out, gq, gk, head_mean, sink = _even_layouts(e_w_in[0], e_w_out[0], e_g_q[0], e_g_k[0], e_sink[0])
    even = (gq, gk, head_mean)
    qkv = _inproj(x, mods, 0, e_g_pre_mix[0], w_in, even=even, rope_tabs=_rope_tables(s), name="inproj0_lat")
    ckv = _inproj(ctx, mods, b, e_g_pre_mix[0], w_in, even=even, rope_tabs=(None, None), name="inproj0_ctx")
    ckv = ckv.reshape(b, n_ctx, -1)
    o_lat = _even_attention(qkv, ckv, sink)
    o_ctx = _even_ctx_attention(ckv, sink).reshape(1, b * n_ctx, -1)
    wg, wu, wd = e_w_gate[0].astype(BF16), e_w_up[0].astype(BF16), e_w_down[0].astype(BF16)
    x, h = _outproj(o_lat, w_out, x, mods, 0, e_g_post_mix[0], e_g_pre_ffn[0], name="outproj0_lat")
    x = _ffn(h, wg, wu, wd, x, mods, 0, e_g_post_ffn[0], name="ffn0_lat")
    ctx, h = _outproj(o_ctx, w_out, ctx, mods, b, e_g_post_mix[0], e_g_pre_ffn[0], name="outproj0_ctx")
    ctx = _ffn(h, wg, wu, wd, ctx, mods, b, e_g_post_ffn[0], name="ffn0_ctx")

    mods = _mod_vectors(c, c_ctx, o_w_mod[0], o_b_mod[0])
    hd = C_HEADS * HEAD_DIM
    w_in = o_w_in[0].astype(BF16)
    qkv = _inproj(x, mods, 0, o_g_pre_mix[0], w_in, n_scaled=hd, name="inproj1_lat")
    ckv = _inproj(ctx, mods, b, o_g_pre_mix[0], w_in[:, hd:], name="inproj1_ctx").reshape(b, n_ctx, 2 * hd)
    o_lat = _na_attention(qkv, ckv, _na_bias_table(o_rpb[0], rows), rows)
    wr = jnp.zeros((d, LANES), F32).at[:, :N_EXPERTS].set(o_w_router[0])
    x, h, route, counts = _outproj(o_lat, o_w_out[0].astype(BF16), x, mods, 0, o_g_post_mix[0], o_g_pre_ffn[0],
                                   router=_split_bf16(wr), name="outproj1_router")
    return _moe(h, route, counts, x, mods, o_g_post_ffn[0],
                o_w_gate[0].astype(BF16), o_w_up[0].astype(BF16), o_w_down[0].astype(BF16))
```

```python
import functools

import numpy as np
import jax
import jax.numpy as jnp
from jax import lax
from jax.experimental import pallas as pl
from jax.experimental.pallas import tpu as pltpu

GRID_W = 64
HEAD_DIM = 64
A_Q_HEADS = 8
A_KV_HEADS = 2
B_Q_HEADS = 8
B_KV_HEADS = 2
C_HEADS = 16
Q_BLOCK = 128
WINDOW = 128
NA_ROWS = 8
NA_COLS = 16
ROPE_THETA = 10000.0
N_EXPERTS = 8
N_MOD = 6
EPS = 1e-6
NEG_INF = -1e30
ATTN_SCALE = HEAD_DIM ** -0.5

LANES = 128
VMEM_LIMIT = 56 * 1024 * 1024

F32 = jnp.float32
BF16 = jnp.bfloat16
HIGHEST = lax.Precision.HIGHEST


def _tile(n, pref):
    return pref if n % pref == 0 else n


def _params(*sem):
    return pltpu.CompilerParams(dimension_semantics=sem, vmem_limit_bytes=VMEM_LIMIT)


def _rms(x):
    return x * lax.rsqrt(jnp.mean(x * x, axis=-1, keepdims=True) + EPS)


def _nt_dot(a, b):
    return lax.dot_general(a, b, (((1,), (1,)), ((), ())), preferred_element_type=F32)


def _dot(a, b):
    return jnp.dot(a, b, preferred_element_type=F32)


def _mod_kernel(c_ref, w_ref, b_ref, o_ref):
    c = c_ref[...]
    a = c / (1.0 + jnp.exp(-c))
    o_ref[...] = jnp.dot(a, w_ref[...], precision=HIGHEST, preferred_element_type=F32) + b_ref[...]


def _mod_vectors(c, c_ctx, w_mod, b_mod):
    b, d = c.shape
    rows = -(-(b + 1) // 8) * 8
    cc = jnp.zeros((rows, d), F32).at[:b].set(c).at[b].set(c_ctx)
    n = w_mod.shape[1]
    tn = _tile(n, 1536)
    m = pl.pallas_call(
        _mod_kernel,
        out_shape=jax.ShapeDtypeStruct((rows, n), F32),
        grid=(n // tn,),
        in_specs=[pl.BlockSpec((rows, d), lambda j: (0, 0)),
                  pl.BlockSpec((d, tn), lambda j: (0, j)),
                  pl.BlockSpec((1, tn), lambda j: (0, j))],
        out_specs=pl.BlockSpec((rows, tn), lambda j: (0, j)),
        compiler_params=_params("arbitrary"),
        name="mod_vectors",
    )(cc, w_mod, b_mod.reshape(1, n))
    return m.reshape(rows, N_MOD, d)


def _modulated(x_ref, m_ref, g_ref):
    x = x_ref[0]
    return (_rms(x) * g_ref[...]) * (1.0 + m_ref[0, 1:2, :]) + m_ref[0, 0:1, :]


def _rope_block(y, cos, sin_signed):
    lane = lax.broadcasted_iota(jnp.int32, y.shape, 1)
    partner = jnp.where(lane % HEAD_DIM < HEAD_DIM // 2,
                        pltpu.roll(y, LANES - HEAD_DIM // 2, 1),
                        pltpu.roll(y, HEAD_DIM // 2, 1))
    return y * cos + partner * sin_signed


def _inproj_even_kernel(x_ref, m_ref, g_ref, w_ref, gq_ref, gk_ref, hm_ref, cos_ref, sin_ref, o_ref, *, rope):
    h = _modulated(x_ref, m_ref, g_ref).astype(BF16)
    y = _dot(h, w_ref[...])
    nq = A_Q_HEADS * HEAD_DIM
    n_qblk = 2 * nq // LANES
    for i in range(y.shape[1] // LANES):
        yb = y[:, i * LANES:(i + 1) * LANES]
        is_qa = i < nq // LANES
        is_q = i < n_qblk
        is_ka = i == n_qblk
        is_kb = i == n_qblk + 2
        if is_qa or is_ka:
            ms = _dot((yb * yb).astype(BF16), hm_ref[...])
            yb = yb * lax.rsqrt(ms + EPS) * (gq_ref[...] if is_qa else gk_ref[...])
        if rope and (is_q or is_ka or is_kb):
            yb = _rope_block(yb, cos_ref[...], sin_ref[...])
        if is_q:
            yb = yb * ATTN_SCALE
        o_ref[0, :, i * LANES:(i + 1) * LANES] = yb.astype(o_ref.dtype)


def _inproj_plain_kernel(x_ref, m_ref, g_ref, w_ref, o_ref, *, n_scaled):
    h = _modulated(x_ref, m_ref, g_ref).astype(BF16)
    y = _dot(h, w_ref[...])
    if n_scaled:
        o_ref[0, :, :n_scaled] = (y[:, :n_scaled] * ATTN_SCALE).astype(o_ref.dtype)
    o_ref[0, :, n_scaled:] = y[:, n_scaled:].astype(o_ref.dtype)


def _inproj(x, mods, mod_off, g, w, *, even=None, rope_tabs=None, n_scaled=0, name):
    grp, n, d = x.shape
    nout = w.shape[1]
    tm = _tile(n, 512)
    grid = (grp, n // tm)
    in_specs = [pl.BlockSpec((1, tm, d), lambda b, t: (b, t, 0)),
                pl.BlockSpec((1, N_MOD, d), lambda b, t: (b + mod_off, 0, 0)),
                pl.BlockSpec((1, d), lambda b, t: (0, 0)),
                pl.BlockSpec((d, nout), lambda b, t: (0, 0))]
    args = [x, mods, g.reshape(1, d), w]
    if even is not None:
        gq, gk, hm = even
        cos, sin = rope_tabs
        rope = cos is not None
        if not rope:
            cos = jnp.zeros((tm, LANES), F32)
            sin = cos
            tab_map = lambda b, t: (0, 0)
        else:
            tab_map = lambda b, t: (t, 0)
        in_specs += [pl.BlockSpec((1, LANES), lambda b, t: (0, 0)),
                     pl.BlockSpec((1, LANES), lambda b, t: (0, 0)),
                     pl.BlockSpec((LANES, LANES), lambda b, t: (0, 0)),
                     pl.BlockSpec((tm, LANES), tab_map),
                     pl.BlockSpec((tm, LANES), tab_map)]
        args += [gq, gk, hm, cos, sin]
        body = functools.partial(_inproj_even_kernel, rope=rope)
    else:
        body = functools.partial(_inproj_plain_kernel, n_scaled=n_scaled)
    return pl.pallas_call(
        body,
        out_shape=jax.ShapeDtypeStruct((grp, n, nout), BF16),
        grid=grid,
        in_specs=in_specs,
        out_specs=pl.BlockSpec((1, tm, nout), lambda b, t: (b, t, 0)),
        compiler_params=_params("parallel", "parallel"),
        name=name,
    )(*args)


def _stack_pair(qp):
    lane = lax.broadcasted_iota(jnp.int32, qp.shape, 1)
    zero = jnp.zeros_like(qp)
    return jnp.concatenate([jnp.where(lane < HEAD_DIM, qp, zero),
                            jnp.where(lane >= HEAD_DIM, qp, zero)], axis=0)


def _unstack_pair(o):
    t = o.shape[0] // 2
    lane = lax.broadcasted_iota(jnp.int32, (t, LANES), 1)
    return jnp.where(lane < HEAD_DIM, o[:t], o[t:])


def _softmax_pv(scores, values, sink=None):
    m = functools.reduce(jnp.maximum, [jnp.max(s, axis=-1, keepdims=True) for s in scores])
    if sink is not None:
        m = jnp.maximum(m, sink)
    den = None
    out = None
    for s, v in zip(scores, values):
        p = jnp.exp(s - m)
        ps = jnp.sum(p, axis=-1, keepdims=True)
        pv = _dot(p.astype(v.dtype), v)
        den = ps if den is None else den + ps
        out = pv if out is None else out + pv
    if sink is not None:
        den = den + jnp.exp(sink - m)
    return out * (1.0 / den)


def _pair_sink(sink_ref, h0, h1, t):
    row = lax.broadcasted_iota(jnp.int32, (2 * t, 1), 0)
    return jnp.where(row < t, sink_ref[h0], sink_ref[h1])


def _pair_heads(p):
    return p, p + A_Q_HEADS // A_KV_HEADS


def _even_attn_kernel(sink_ref, q_ref, kv_ref, ckv_ref, o_ref, *, seq, band):
    j = pl.program_id(1)
    tq = q_ref.shape[1]
    n_pairs = A_Q_HEADS * HEAD_DIM // LANES
    cka, cva = ckv_ref[0, :, 0:LANES], ckv_ref[0, :, LANES:2 * LANES]
    ckb, cvb = ckv_ref[0, :, 2 * LANES:3 * LANES], ckv_ref[0, :, 3 * LANES:4 * LANES]
    ka, va = kv_ref[0, :, 0:LANES], kv_ref[0, :, LANES:2 * LANES]
    for p in range(n_pairs):
        qs = _stack_pair(q_ref[0, :, p * LANES:(p + 1) * LANES])
        o = _softmax_pv([_nt_dot(qs, ka), _nt_dot(qs, cka)], [va, cva])
        o_ref[0, :, p * LANES:(p + 1) * LANES] = _unstack_pair(o).astype(o_ref.dtype)
    start = pl.multiple_of(jnp.clip((j - 1) * tq, 0, seq - band), Q_BLOCK)
    kb = kv_ref[0, pl.ds(start, band), 2 * LANES:3 * LANES]
    vb = kv_ref[0, pl.ds(start, band), 3 * LANES:4 * LANES]
    qpos = j * tq + lax.broadcasted_iota(jnp.int32, (tq, band), 0)
    kpos = start + lax.broadcasted_iota(jnp.int32, (tq, band), 1)
    ok = jnp.abs(kpos - qpos) <= WINDOW
    ok = jnp.concatenate([ok, ok], axis=0)
    off = n_pairs * LANES
    for p in range(n_pairs):
        h0, h1 = _pair_heads(p)
        qs = _stack_pair(q_ref[0, :, off + p * LANES:off + (p + 1) * LANES])
        s_win = jnp.where(ok, _nt_dot(qs, kb), NEG_INF)
        o = _softmax_pv([s_win, _nt_dot(qs, ckb)], [vb, cvb], _pair_sink(sink_ref, h0, h1, tq))
        o_ref[0, :, off + p * LANES:off + (p + 1) * LANES] = _unstack_pair(o).astype(o_ref.dtype)


def _even_attention(qkv, ckv, sink):
    b, s, _ = qkv.shape
    c = ckv.shape[1]
    tq = Q_BLOCK
    nq = 2 * A_Q_HEADS * HEAD_DIM
    nkv = 4 * LANES
    band = min(3 * Q_BLOCK, s)
    return pl.pallas_call(
        functools.partial(_even_attn_kernel, seq=s, band=band),
        out_shape=jax.ShapeDtypeStruct((b, s, nq), BF16),
        grid=(b, s // tq),
        in_specs=[pl.BlockSpec(memory_space=pltpu.SMEM),
                  pl.BlockSpec((1, tq, nq), lambda i, j: (i, j, 0)),
                  pl.BlockSpec((1, s, nkv), lambda i, j: (i, 0, nq // nkv)),
                  pl.BlockSpec((1, c, nkv), lambda i, j: (i, 0, nq // nkv))],
        out_specs=pl.BlockSpec((1, tq, nq), lambda i, j: (i, j, 0)),
        compiler_params=_params("parallel", "parallel"),
        name="even_attention",
    )(sink, qkv, qkv, ckv)


def _even_ctx_attn_kernel(sink_ref, q_ref, ckv_ref, o_ref):
    tq = q_ref.shape[1]
    n_pairs = A_Q_HEADS * HEAD_DIM // LANES
    cka, cva = ckv_ref[0, :, 0:LANES], ckv_ref[0, :, LANES:2 * LANES]
    ckb, cvb = ckv_ref[0, :, 2 * LANES:3 * LANES], ckv_ref[0, :, 3 * LANES:4 * LANES]
    off = n_pairs * LANES
    for p in range(n_pairs):
        qs = _stack_pair(q_ref[0, :, p * LANES:(p + 1) * LANES])
        o = _softmax_pv([_nt_dot(qs, cka)], [cva])
        o_ref[0, :, p * LANES:(p + 1) * LANES] = _unstack_pair(o).astype(o_ref.dtype)
        h0, h1 = _pair_heads(p)
        qs = _stack_pair(q_ref[0, :, off + p * LANES:off + (p + 1) * LANES])
        o = _softmax_pv([_nt_dot(qs, ckb)], [cvb], _pair_sink(sink_ref, h0, h1, tq))
        o_ref[0, :, off + p * LANES:off + (p + 1) * LANES] = _unstack_pair(o).astype(o_ref.dtype)


def _even_ctx_attention(ckv, sink):
    b, c, _ = ckv.shape
    nq = 2 * A_Q_HEADS * HEAD_DIM
    nkv = 4 * LANES
    return pl.pallas_call(
        _even_ctx_attn_kernel,
        out_shape=jax.ShapeDtypeStruct((b, c, nq), BF16),
        grid=(b,),
        in_specs=[pl.BlockSpec(memory_space=pltpu.SMEM),
                  pl.BlockSpec((1, c, nq), lambda i: (i, 0, 0)),
                  pl.BlockSpec((1, c, nkv), lambda i: (i, 0, nq // nkv))],
        out_specs=pl.BlockSpec((1, c, nq), lambda i: (i, 0, 0)),
        compiler_params=_params("parallel"),
        name="even_ctx_attention",
    )(sink, ckv, ckv)


def _na_kernel(q_ref, k_ref, v_ref, ck_ref, cv_ref, bias_ref, o_ref, *, rows, kh):
    r = pl.program_id(1)
    rs = jnp.clip(r - kh // 2, 0, rows - kh)
    start = pl.multiple_of(rs * GRID_W, GRID_W)
    n_loc = kh * GRID_W
    for p in range(C_HEADS * HEAD_DIM // LANES):
        sl = slice(p * LANES, (p + 1) * LANES)
        qs = _stack_pair(q_ref[0, :, sl])
        kb = k_ref[0, pl.ds(start, n_loc), sl]
        vb = v_ref[0, pl.ds(start, n_loc), sl]
        s_loc = _nt_dot(qs, kb) + bias_ref[0, p]
        o = _softmax_pv([s_loc, _nt_dot(qs, ck_ref[0, :, sl])], [vb, cv_ref[0, :, sl]])
        o_ref[0, :, sl] = _unstack_pair(o).astype(o_ref.dtype)


def _na_bias_table(rpb, rows):
    kh = min(NA_ROWS, rows)
    cols = np.arange(GRID_W)
    col_start = np.clip(cols - NA_COLS // 2, 0, GRID_W - NA_COLS)
    col_in = (cols[None, :] >= col_start[:, None]) & (cols[None, :] < col_start[:, None] + NA_COLS)
    pad = GRID_W - NA_COLS
    padded = jnp.pad(rpb.astype(F32), ((0, 0), (0, 0), (pad, pad)), mode="edge")
    t = jnp.stack([padded[:, :, GRID_W - 1 - c:2 * GRID_W - 1 - c] for c in range(GRID_W)], axis=2)
    t = jnp.where(col_in[None, None], t, NEG_INF)
    bands = []
    for d in range(kh):
        band = t[:, NA_ROWS - 1 - d:NA_ROWS - 1 - d + kh]
        bands.append(jnp.swapaxes(band, 1, 2).reshape(C_HEADS // 2, 2 * GRID_W, kh * GRID_W))
    return jnp.stack(bands, axis=0)


def _na_attention(qkv, ckv, bias, rows):
    b, s, _ = qkv.shape
    c = ckv.shape[1]
    hd = C_HEADS * HEAD_DIM
    kh = min(NA_ROWS, rows)

    def bias_map(i, r):
        return (r - jnp.clip(r - kh // 2, 0, rows - kh), 0, 0, 0)

    return pl.pallas_call(
        functools.partial(_na_kernel, rows=rows, kh=kh),
        out_shape=jax.ShapeDtypeStruct((b, s, hd), BF16),
        grid=(b, rows),
        in_specs=[pl.BlockSpec((1, GRID_W, hd), lambda i, r: (i, r, 0)),
                  pl.BlockSpec((1, s, hd), lambda i, r: (i, 0, 1)),
                  pl.BlockSpec((1, s, hd), lambda i, r: (i, 0, 2)),
                  pl.BlockSpec((1, c, hd), lambda i, r: (i, 0, 0)),
                  pl.BlockSpec((1, c, hd), lambda i, r: (i, 0, 1)),
                  pl.BlockSpec((1,) + bias.shape[1:], bias_map)],
        out_specs=pl.BlockSpec((1, GRID_W, hd), lambda i, r: (i, r, 0)),
        compiler_params=_params("parallel", "parallel"),
        name="na_attention",
    )(qkv, qkv, qkv, ckv, ckv, bias)


def _outproj_core(o_ref, w_ref, x_ref, m_ref, gpost_ref, gpre_ref):
    y = _dot(o_ref[0], w_ref[...])
    x = x_ref[0] + m_ref[0, 2:3, :] * (_rms(y) * gpost_ref[...])
    h = (_rms(x) * gpre_ref[...]) * (1.0 + m_ref[0, 4:5, :]) + m_ref[0, 3:4, :]
    return x, h


def _outproj_kernel(o_ref, w_ref, x_ref, m_ref, gpost_ref, gpre_ref, xo_ref, h_ref):
    x, h = _outproj_core(o_ref, w_ref, x_ref, m_ref, gpost_ref, gpre_ref)
    xo_ref[0] = x
    h_ref[0] = h.astype(h_ref.dtype)


def _split_bf16(a):
    hi = a.astype(BF16)
    return hi, (a - hi.astype(F32)).astype(BF16)


def _outproj_router_kernel(o_ref, w_ref, x_ref, m_ref, gpost_ref, gpre_ref, wr_hi_ref, wr_lo_ref,
                           xo_ref, h_ref, route_ref, cnt_ref, carry_ref):
    first = (pl.program_id(0) == 0) & (pl.program_id(1) == 0)

    @pl.when(first)
    def _():
        carry_ref[...] = jnp.zeros_like(carry_ref)

    x, h = _outproj_core(o_ref, w_ref, x_ref, m_ref, gpost_ref, gpre_ref)
    xo_ref[0] = x
    h_ref[0] = h
    h_hi, h_lo = _split_bf16(h)
    logits = _dot(h_hi, wr_hi_ref[...]) + (_dot(h_hi, wr_lo_ref[...]) + _dot(h_lo, wr_hi_ref[...]))
    t = logits.shape[0]
    lane = lax.broadcasted_iota(jnp.int32, logits.shape, 1).astype(F32)
    logits = jnp.where(lane < N_EXPERTS, logits, -jnp.inf)
    v1 = jnp.max(logits, axis=-1, keepdims=True)
    i1 = jnp.min(jnp.where(logits == v1, lane, float(LANES)), axis=-1, keepdims=True)
    rest = jnp.where(lane == i1, -jnp.inf, logits)
    v2 = jnp.max(rest, axis=-1, keepdims=True)
    i2 = jnp.min(jnp.where(rest == v2, lane, float(LANES)), axis=-1, keepdims=True)
    e2 = jnp.exp(v2 - v1)
    w1 = 1.0 / (1.0 + e2)
    w2 = e2 / (1.0 + e2)
    sel = ((lane == i1) | (lane == i2)).astype(F32)
    rr = lax.broadcasted_iota(jnp.int32, (t, t), 0)
    cc = lax.broadcasted_iota(jnp.int32, (t, t), 1)
    before = _dot(jnp.where(rr > cc, 1.0, 0.0).astype(BF16), sel.astype(BF16)) + carry_ref[...]
    r1 = jnp.sum(jnp.where(lane == i1, before, 0.0), axis=-1, keepdims=True)
    r2 = jnp.sum(jnp.where(lane == i2, before, 0.0), axis=-1, keepdims=True)
    carry = carry_ref[...] + jnp.sum(sel, axis=0, keepdims=True)
    carry_ref[...] = carry
    cnt_ref[...] = carry
    fields = [i1, i2, w1, w2, r1, r2]
    route = jnp.zeros((t, LANES), F32)
    for k, f in enumerate(fields):
        route = jnp.where(lane == k, f, route)
    route_ref[0] = route[:, :route_ref.shape[2]]


def _outproj(o, w, x, mods, mod_off, g_post, g_pre, *, router=None, name):
    grp, n, d = x.shape
    dm = o.shape[2]
    tm = _tile(n, 512)
    tok = lambda b, t: (b, t, 0)
    const = lambda b, t: (0, 0)
    in_specs = [pl.BlockSpec((1, tm, dm), tok),
                pl.BlockSpec((dm, d), const),
                pl.BlockSpec((1, tm, d), tok),
                pl.BlockSpec((1, N_MOD, d), lambda b, t: (b + mod_off, 0, 0)),
                pl.BlockSpec((1, d), const),
                pl.BlockSpec((1, d), const)]
    args = [o, w, x, mods, g_post.reshape(1, d), g_pre.reshape(1, d)]
    if router is None:
        return pl.pallas_call(
            _outproj_kernel,
            out_shape=(jax.ShapeDtypeStruct((grp, n, d), F32), jax.ShapeDtypeStruct((grp, n, d), BF16)),
            grid=(grp, n // tm),
            in_specs=in_specs,
            out_specs=(pl.BlockSpec((1, tm, d), tok), pl.BlockSpec((1, tm, d), tok)),
            compiler_params=_params("parallel", "parallel"),
            name=name,
        )(*args)
    wr_hi, wr_lo = router
    in_specs += [pl.BlockSpec((d, LANES), const), pl.BlockSpec((d, LANES), const)]
    args += [wr_hi, wr_lo]
    return pl.pallas_call(
        _outproj_router_kernel,
        out_shape=(jax.ShapeDtypeStruct((grp, n, d), F32), jax.ShapeDtypeStruct((grp, n, d), F32),
                   jax.ShapeDtypeStruct((grp, n, 8), F32), jax.ShapeDtypeStruct((1, LANES), F32)),
        grid=(grp, n // tm),
        in_specs=in_specs,
        out_specs=(pl.BlockSpec((1, tm, d), tok), pl.BlockSpec((1, tm, d), tok),
                   pl.BlockSpec((1, tm, 8), tok), pl.BlockSpec((1, LANES), const)),
        scratch_shapes=[pltpu.VMEM((1, LANES), F32)],
        compiler_params=_params("arbitrary", "arbitrary"),
        name=name,
    )(*args)


def _swiglu_chunk(h, wg, wu, wd):
    g = _dot(h, wg)
    u = _dot(h, wu)
    a = (g / (1.0 + jnp.exp(-g))) * u
    return _dot(a.astype(BF16), wd)


def _ffn_kernel(h_ref, wg_ref, wu_ref, wd_ref, x_ref, m_ref, g_ref, o_ref, acc_ref):
    c = pl.program_id(2)
    part = _swiglu_chunk(h_ref[0], wg_ref[...], wu_ref[...], wd_ref[...])

    @pl.when(c == 0)
    def _():
        acc_ref[...] = part

    @pl.when(c > 0)
    def _():
        acc_ref[...] += part

    @pl.when(c == pl.num_programs(2) - 1)
    def _():
        o_ref[0] = x_ref[0] + m_ref[0, 5:6, :] * (_rms(acc_ref[...]) * g_ref[...])


def _ffn(h, wg, wu, wd, x, mods, mod_off, g_post, *, name):
    grp, n, d = x.shape
    ff = wg.shape[1]
    tm = _tile(n, 512)
    fc = _tile(ff, 1408)
    tok = lambda b, t, c: (b, t, 0)
    return pl.pallas_call(
        _ffn_kernel,
        out_shape=jax.ShapeDtypeStruct((grp, n, d), F32),
        grid=(grp, n // tm, ff // fc),
        in_specs=[pl.BlockSpec((1, tm, d), tok),
                  pl.BlockSpec((d, fc), lambda b, t, c: (0, c)),
                  pl.BlockSpec((d, fc), lambda b, t, c: (0, c)),
                  pl.BlockSpec((fc, d), lambda b, t, c: (c, 0)),
                  pl.BlockSpec((1, tm, d), tok),
                  pl.BlockSpec((1, N_MOD, d), lambda b, t, c: (b + mod_off, 0, 0)),
                  pl.BlockSpec((1, d), lambda b, t, c: (0, 0))],
        out_specs=pl.BlockSpec((1, tm, d), tok),
        scratch_shapes=[pltpu.VMEM((tm, d), F32)],
        compiler_params=_params("parallel", "parallel", "arbitrary"),
        name=name,
    )(h, wg, wu, wd, x, mods, g_post.reshape(1, d))


def _dispatch_kernel(pos_ref, h_ref, xs_in_ref, xs_ref, sem):
    del xs_in_ref
    tm = h_ref.shape[0]

    def copy(t, k):
        return pltpu.make_async_copy(h_ref.at[pl.ds(t, 1)], xs_ref.at[pl.ds(pos_ref[2 * t + k], 1)], sem)

    def issue(t, carry):
        copy(t, 0).start()
        copy(t, 1).start()
        return carry

    def drain(t, carry):
        copy(t, 0).wait()
        copy(t, 1).wait()
        return carry

    lax.fori_loop(0, tm, issue, 0)
    lax.fori_loop(0, tm, drain, 0)


def _dispatch(h, pos_flat, n_rows):
    n, d = h.shape
    tm = _tile(n, 512)
    zeros = jnp.zeros((n_rows, d), h.dtype)
    return pl.pallas_call(
        _dispatch_kernel,
        out_shape=jax.ShapeDtypeStruct((n_rows, d), h.dtype),
        grid=(n // tm,),
        in_specs=[pl.BlockSpec((2 * tm,), lambda i: (i,), memory_space=pltpu.SMEM),
                  pl.BlockSpec((tm, d), lambda i: (i, 0)),
                  pl.BlockSpec(memory_space=pl.ANY)],
        out_specs=pl.BlockSpec(memory_space=pl.ANY),
        scratch_shapes=[pltpu.SemaphoreType.DMA],
        input_output_aliases={2: 0},
        compiler_params=_params("arbitrary"),
        name="moe_dispatch",
    )(pos_flat, h, zeros)


def _experts_kernel(te_ref, nt_ref, xs_ref, wg_ref, wu_ref, wd_ref, ys_ref, xb_ref):
    i = pl.program_id(0)
    c = pl.program_id(1)

    @pl.when(i < nt_ref[0])
    def _():
        @pl.when(c == 0)
        def _():
            xb_ref[...] = xs_ref[...].astype(BF16)

        part = _swiglu_chunk(xb_ref[...], wg_ref[0], wu_ref[0], wd_ref[0])

        @pl.when(c == 0)
        def _():
            ys_ref[...] = part

        @pl.when(c > 0)
        def _():
            ys_ref[...] += part

    @pl.when((i >= nt_ref[0]) & (c == 0))
    def _():
        ys_ref[...] = jnp.zeros_like(ys_ref)


def _experts(xs, tile_expert, n_tiles, wg, wu, wd, tmg):
    n_rows, d = xs.shape
    ff = wg.shape[2]
    fc = _tile(ff, 896)
    total = n_rows // tmg

    def row_map(i, c, te, nt):
        return (jnp.minimum(i, nt[0] - 1), 0)

    def w_in_map(i, c, te, nt):
        return (te[jnp.minimum(i, nt[0] - 1)], 0, jnp.where(i < nt[0], c, ff // fc - 1))

    def w_out_map(i, c, te, nt):
        return (te[jnp.minimum(i, nt[0] - 1)], jnp.where(i < nt[0], c, ff // fc - 1), 0)

    return pl.pallas_call(
        _experts_kernel,
        out_shape=jax.ShapeDtypeStruct((n_rows, d), F32),
        grid_spec=pltpu.PrefetchScalarGridSpec(
            num_scalar_prefetch=2,
            grid=(total, ff // fc),
            in_specs=[pl.BlockSpec((tmg, d), row_map),
                      pl.BlockSpec((1, d, fc), w_in_map),
                      pl.BlockSpec((1, d, fc), w_in_map),
                      pl.BlockSpec((1, fc, d), w_out_map)],
            out_specs=pl.BlockSpec((tmg, d), lambda i, c, te, nt: (i, 0)),
            scratch_shapes=[pltpu.VMEM((tmg, d), BF16)]),
        compiler_params=_params("arbitrary", "arbitrary"),
        name="moe_experts",
    )(tile_expert, n_tiles, xs, wg, wu, wd)


def _combine_kernel(pos_ref, ys_ref, route_ref, x_ref, m_ref, g_ref, o_ref, buf_ref, sem):
    tm = x_ref.shape[1]

    def copy(t, k):
        return pltpu.make_async_copy(ys_ref.at[pl.ds(pos_ref[2 * t + k], 1)], buf_ref.at[k, pl.ds(t, 1)], sem)

    def issue(t, carry):
        copy(t, 0).start()
        copy(t, 1).start()
        return carry

    def drain(t, carry):
        copy(t, 0).wait()
        copy(t, 1).wait()
        return carry

    lax.fori_loop(0, tm, issue, 0)
    lax.fori_loop(0, tm, drain, 0)
    route = route_ref[0]
    y = route[:, 2:3] * buf_ref[0] + route[:, 3:4] * buf_ref[1]
    o_ref[0] = x_ref[0] + m_ref[0, 5:6, :] * (_rms(y) * g_ref[...])


def _combine(ys, pos_flat, route, x, mods, g_post):
    b, s, d = x.shape
    tm = _tile(s, 512)
    per = s // tm
    tok = lambda i, t: (i, t, 0)
    return pl.pallas_call(
        _combine_kernel,
        out_shape=jax.ShapeDtypeStruct((b, s, d), F32),
        grid=(b, per),
        in_specs=[pl.BlockSpec((2 * tm,), lambda i, t: (i * per + t,), memory_space=pltpu.SMEM),
                  pl.BlockSpec(memory_space=pl.ANY),
                  pl.BlockSpec((1, tm, 8), tok),
                  pl.BlockSpec((1, tm, d), tok),
                  pl.BlockSpec((1, N_MOD, d), lambda i, t: (i, 0, 0)),
                  pl.BlockSpec((1, d), lambda i, t: (0, 0))],
        out_specs=pl.BlockSpec((1, tm, d), tok),
        scratch_shapes=[pltpu.VMEM((2, tm, d), F32), pltpu.SemaphoreType.DMA],
        compiler_params=_params("arbitrary", "arbitrary"),
        name="moe_combine",
    )(pos_flat, ys, route, x, mods, g_post.reshape(1, d))


def _moe(h, route, counts, x, mods, g_post, wg, wu, wd):
    b, s, d = x.shape
    n = b * s
    tmg = _tile(2 * n, 1024)
    total_tiles = 2 * n // tmg + N_EXPERTS
    cnt = counts[0, :N_EXPERTS].astype(jnp.int32)
    tiles_per = (cnt + tmg - 1) // tmg
    tile_end = jnp.cumsum(tiles_per)
    offsets = (tile_end - tiles_per) * tmg
    r = route.reshape(n, 8)
    idx = r[:, 0:2].astype(jnp.int32)
    pos = offsets[idx] + r[:, 4:6].astype(jnp.int32)
    pos_flat = pos.reshape(2 * n)
    tile_expert = jnp.sum((tile_end[None, :] <= jnp.arange(total_tiles)[:, None]).astype(jnp.int32), axis=1)
    tile_expert = jnp.minimum(tile_expert, N_EXPERTS - 1)
    n_tiles = tile_end[-1:].astype(jnp.int32)
    xs = _dispatch(h.reshape(n, d), pos_flat, total_tiles * tmg)
    ys = _experts(xs, tile_expert, n_tiles, wg, wu, wd, tmg)
    return _combine(ys, pos_flat, route, x, mods, g_post)


def _even_layouts(w_in, w_out, g_q, g_k, sink):
    dh, half = HEAD_DIM, HEAD_DIM // 2
    within = np.concatenate([np.arange(0, dh, 2), np.arange(1, dh, 2)])
    g = A_Q_HEADS // A_KV_HEADS
    q_order = np.array([h for p in range(g) for h in _pair_heads(p)])
    sizes = [A_Q_HEADS * dh, A_KV_HEADS * dh, A_KV_HEADS * dh, B_Q_HEADS * dh, B_KV_HEADS * dh, B_KV_HEADS * dh]
    base = np.concatenate([[0], np.cumsum(sizes)])

    def head_cols(start, order, permute):
        inner = within if permute else np.arange(dh)
        return np.concatenate([start + h * dh + inner for h in order])

    kv_order = np.arange(A_KV_HEADS)
    cols = np.concatenate([head_cols(base[0], q_order, True), head_cols(base[3], q_order, True),
                           head_cols(base[1], kv_order, True), head_cols(base[2], kv_order, False),
                           head_cols(base[4], kv_order, True), head_cols(base[5], kv_order, False)])
    rows_out = np.concatenate([head_cols(0, q_order, False), head_cols(A_Q_HEADS * dh, q_order, False)])
    gq = jnp.tile(g_q[within], LANES // dh).reshape(1, LANES)
    gk = jnp.tile(g_k[within], LANES // dh).reshape(1, LANES)
    lane = np.arange(LANES)
    head_mean = jnp.asarray((lane[:, None] // dh == lane[None, :] // dh) / dh, BF16)
    return (w_in[:, cols].astype(BF16), w_out[rows_out].astype(BF16), gq, gk, head_mean,
            sink.astype(F32))


def _rope_tables(n_tokens):
    t = jnp.arange(n_tokens, dtype=jnp.int32)
    row = (t // GRID_W).astype(F32)
    col = (t % GRID_W).astype(F32)
    half = HEAD_DIM // 2
    freqs = ROPE_THETA ** (-jnp.arange(0, half, 2, dtype=F32) / half)
    ang = jnp.concatenate([row[:, None] * freqs, col[:, None] * freqs], axis=-1)
    cos, sin = jnp.cos(ang), jnp.sin(ang)
    reps = LANES // HEAD_DIM
    return (jnp.tile(jnp.concatenate([cos, cos], axis=-1), (1, reps)),
            jnp.tile(jnp.concatenate([-sin, sin], axis=-1), (1, reps)))


def kernel(x, c, ctx, c_ctx, e_w_mod, e_b_mod, e_g_pre_mix, e_g_post_mix, e_g_pre_ffn, e_g_post_ffn, e_w_in, e_w_out, e_g_q, e_g_k, e_sink, e_w_gate, e_w_up, e_w_down, o_w_mod, o_b_mod, o_g_pre_mix, o_g_post_mix, o_g_pre_ffn, o_g_post_ffn, o_w_in, o_w_out, o_rpb, o_w_router, o_w_gate, o_w_up, o_w_down):
    b, s, d = x.shape
    n_ctx = ctx.shape[1]
    rows = s // GRID_W
    ctx = ctx.reshape(1, b * n_ctx, d)

    mods = _mod_vectors(c, c_ctx, e_w_mod[0], e_b_mod[0])
    w_in, w_out, gq, gk, head_mean, sink = _even_layouts(e_w_in[0], e_w_out[0], e_g_q[0], e_g_k[0], e_sink[0])
    even = (gq, gk, head_mean)
    qkv = _inproj(x, mods, 0, e_g_pre_mix[0], w_in, even=even, rope_tabs=_rope_tables(s), name="inproj0_lat")
    ckv = _inproj(ctx, mods, b, e_g_pre_mix[0], w_in, even=even, rope_tabs=(None, None), name="inproj0_ctx")
    ckv = ckv.reshape(b, n_ctx, -1)
    o_lat = _even_attention(qkv, ckv, sink)
    o_ctx = _even_ctx_attention(ckv, sink).reshape(1, b * n_ctx, -1)
    wg, wu, wd = e_w_gate[0].astype(BF16), e_w_up[0].astype(BF16), e_w_down[0].astype(BF16)
    x, h = _outproj(o_lat, w_out, x, mods, 0, e_g_post_mix[0], e_g_pre_ffn[0], name="outproj0_lat")
    x = _ffn(h, wg, wu, wd, x, mods, 0, e_g_post_ffn[0], name="ffn0_lat")
    ctx, h = _outproj(o_ctx, w_out, ctx, mods, b, e_g_post_mix[0], e_g_pre_ffn[0], name="outproj0_ctx")
    ctx = _ffn(h, wg, wu, wd, ctx, mods, b, e_g_post_ffn[0], name="ffn0_ctx")

    mods = _mod_vectors(c, c_ctx, o_w_mod[0], o_b_mod[0])
    hd = C_HEADS * HEAD_DIM
    w_in = o_w_in[0].astype(BF16)
    qkv = _inproj(x, mods, 0, o_g_pre_mix[0], w_in, n_scaled=hd, name="inproj1_lat")
    ckv = _inproj(ctx, mods, b, o_g_pre_mix[0], w_in[:, hd:], name="inproj1_ctx").reshape(b, n_ctx, 2 * hd)
    o_lat = _na_attention(qkv, ckv, _na_bias_table(o_rpb[0], rows), rows)
    wr = jnp.zeros((d, LANES), F32).at[:, :N_EXPERTS].set(o_w_router[0])
    x, h, route, counts = _outproj(o_lat, o_w_out[0].astype(BF16), x, mods, 0, o_g_post_mix[0], o_g_pre_ffn[0],
                                   router=_split_bf16(wr), name="outproj1_router")
    return _moe(h, route, counts, x, mods, o_g_post_ffn[0],
                o_w_gate[0].astype(BF16), o_w_up[0].astype(BF16), o_w_down[0].astype(BF16))
```

```python
import functools

import numpy as np
import jax
import jax.numpy as jnp
from jax import lax
from jax.experimental import pallas as pl
from jax.experimental.pallas import tpu as pltpu

GRID_W = 64
HEAD_DIM = 64
A_Q_HEADS = 8
A_KV_HEADS = 2
B_Q_HEADS = 8
B_KV_HEADS = 2
C_HEADS = 16
Q_BLOCK = 128
WINDOW = 128
NA_ROWS = 8
NA_COLS = 16
ROPE_THETA = 10000.0
N_EXPERTS = 8
N_MOD = 6
EPS = 1e-6
NEG_INF = -1e30
ATTN_SCALE = HEAD_DIM ** -0.5
LOG2E = 1.4426950408889634
Q_SCALE = ATTN_SCALE * LOG2E

LANES = 128
VMEM_LIMIT = 56 * 1024 * 1024

F32 = jnp.float32
BF16 = jnp.bfloat16
HIGHEST = lax.Precision.HIGHEST


def _tile(n, pref):
    return pref if n % pref == 0 else n


def _params(*sem):
    return pltpu.CompilerParams(dimension_semantics=sem, vmem_limit_bytes=VMEM_LIMIT)


def _rms(x):
    return x * lax.rsqrt(jnp.mean(x * x, axis=-1, keepdims=True) + EPS)


def _nt_dot(a, b):
    return lax.dot_general(a, b, (((1,), (1,)), ((), ())), preferred_element_type=F32)


def _dot(a, b):
    return jnp.dot(a, b, preferred_element_type=F32)


def _mod_kernel(c_ref, w_ref, b_ref, o_ref):
    c = c_ref[...]
    a = c / (1.0 + jnp.exp(-c))
    o_ref[...] = jnp.dot(a, w_ref[...], precision=HIGHEST, preferred_element_type=F32) + b_ref[...]


def _mod_vectors(c, c_ctx, w_mod, b_mod):
    b, d = c.shape
    rows = -(-(b + 1) // 8) * 8
    cc = jnp.zeros((rows, d), F32).at[:b].set(c).at[b].set(c_ctx)
    n = w_mod.shape[1]
    tn = _tile(n, 1536)
    m = pl.pallas_call(
        _mod_kernel,
        out_shape=jax.ShapeDtypeStruct((rows, n), F32),
        grid=(n // tn,),
        in_specs=[pl.BlockSpec((rows, d), lambda j: (0, 0)),
                  pl.BlockSpec((d, tn), lambda j: (0, j)),
                  pl.BlockSpec((1, tn), lambda j: (0, j))],
        out_specs=pl.BlockSpec((rows, tn), lambda j: (0, j)),
        compiler_params=_params("arbitrary"),
        name="mod_vectors",
    )(cc, w_mod, b_mod.reshape(1, n))
    return m.reshape(rows, N_MOD, d)


def _modulated(x_ref, m_ref, g_ref):
    x = x_ref[0]
    return (_rms(x) * g_ref[...]) * (1.0 + m_ref[0, 1:2, :]) + m_ref[0, 0:1, :]


def _rope_block(y, cos, sin_signed):
    lane = lax.broadcasted_iota(jnp.int32, y.shape, 1)
    partner = jnp.where(lane % HEAD_DIM < HEAD_DIM // 2,
                        pltpu.roll(y, LANES - HEAD_DIM // 2, 1),
                        pltpu.roll(y, HEAD_DIM // 2, 1))
    return y * cos + partner * sin_signed


def _store_value_block(o_ref, col, vb):
    o_ref[0, :, col:col + LANES] = vb.astype(o_ref.dtype)
    o_ref[0, :, col + LANES:col + 2 * LANES] = jnp.ones(vb.shape, o_ref.dtype)


def _inproj_even_kernel(x_ref, m_ref, g_ref, w_ref, gq_ref, gk_ref, hm_ref, cos_ref, sin_ref, o_ref, *, rope):
    h = _modulated(x_ref, m_ref, g_ref).astype(BF16)
    y = _dot(h, w_ref[...])
    n_qa = A_Q_HEADS * HEAD_DIM // LANES
    n_q = n_qa + B_Q_HEADS * HEAD_DIM // LANES
    for i in range(y.shape[1] // LANES):
        yb = y[:, i * LANES:(i + 1) * LANES]
        is_qa, is_q, is_ka, is_k = i < n_qa, i < n_q, i == n_q, n_q <= i < n_q + 2
        if is_qa or is_ka:
            ms = _dot((yb * yb).astype(BF16), hm_ref[...])
            yb = yb * lax.rsqrt(ms + EPS) * (gq_ref[...] if is_qa else gk_ref[...])
        if rope and (is_q or is_k):
            yb = _rope_block(yb, cos_ref[...], sin_ref[...])
        if is_q:
            yb = yb * Q_SCALE
        if is_q or is_k:
            o_ref[0, :, i * LANES:(i + 1) * LANES] = yb.astype(o_ref.dtype)
        else:
            _store_value_block(o_ref, (n_q + 2 + 2 * (i - n_q - 2)) * LANES, yb)


def _inproj_plain_kernel(x_ref, m_ref, g_ref, w_ref, o_ref, *, segments):
    h = _modulated(x_ref, m_ref, g_ref).astype(BF16)
    y = _dot(h, w_ref[...])
    src = dst = 0
    for kind, width in segments:
        seg = y[:, src:src + width]
        if kind == "v":
            for p in range(width // LANES):
                _store_value_block(o_ref, dst + 2 * p * LANES, seg[:, p * LANES:(p + 1) * LANES])
            dst += 2 * width
        else:
            o_ref[0, :, dst:dst + width] = (seg * Q_SCALE if kind == "q" else seg).astype(o_ref.dtype)
            dst += width
        src += width


def _inproj(x, mods, mod_off, g, w, *, even=None, rope_tabs=None, segments=None, name):
    grp, n, d = x.shape
    if even is not None:
        nout = w.shape[1] + (A_KV_HEADS + B_KV_HEADS) * HEAD_DIM
    else:
        nout = sum(width * (2 if kind == "v" else 1) for kind, width in segments)
    tm = _tile(n, 512)
    grid = (grp, n // tm)
    in_specs = [pl.BlockSpec((1, tm, d), lambda b, t: (b, t, 0)),
                pl.BlockSpec((1, N_MOD, d), lambda b, t: (b + mod_off, 0, 0)),
                pl.BlockSpec((1, d), lambda b, t: (0, 0)),
                pl.BlockSpec(w.shape, lambda b, t: (0, 0))]
    args = [x, mods, g.reshape(1, d), w]
    if even is not None:
        gq, gk, hm = even
        cos, sin = rope_tabs
        rope = cos is not None
        if not rope:
            cos = jnp.zeros((tm, LANES), F32)
            sin = cos
            tab_map = lambda b, t: (0, 0)
        else:
            tab_map = lambda b, t: (t, 0)
        in_specs += [pl.BlockSpec((1, LANES), lambda b, t: (0, 0)),
                     pl.BlockSpec((1, LANES), lambda b, t: (0, 0)),
                     pl.BlockSpec((LANES, LANES), lambda b, t: (0, 0)),
                     pl.BlockSpec((tm, LANES), tab_map),
                     pl.BlockSpec((tm, LANES), tab_map)]
        args += [gq, gk, hm, cos, sin]
        body = functools.partial(_inproj_even_kernel, rope=rope)
    else:
        body = functools.partial(_inproj_plain_kernel, segments=segments)
    return pl.pallas_call(
        body,
        out_shape=jax.ShapeDtypeStruct((grp, n, nout), BF16),
        grid=grid,
        in_specs=in_specs,
        out_specs=pl.BlockSpec((1, tm, nout), lambda b, t: (b, t, 0)),
        compiler_params=_params("parallel", "parallel"),
        name=name,
    )(*args)


def _stack_pair(qp):
    lane = lax.broadcasted_iota(jnp.int32, qp.shape, 1)
    zero = jnp.zeros_like(qp)
    return jnp.concatenate([jnp.where(lane < HEAD_DIM, qp, zero),
                            jnp.where(lane >= HEAD_DIM, qp, zero)], axis=0)


def _unstack_pair(o):
    t = o.shape[0] // 2
    lane = lax.broadcasted_iota(jnp.int32, (t, LANES), 1)
    return jnp.where(lane < HEAD_DIM, o[:t], o[t:])


def _softmax_pv(scores, values, sink=None):
    m = functools.reduce(jnp.maximum, [jnp.max(s, axis=-1, keepdims=True) for s in scores])
    if sink is not None:
        m = jnp.maximum(m, sink)
    acc = None
    for s, v in zip(scores, values):
        pv = _dot(jnp.exp2(s - m).astype(v.dtype), v)
        acc = pv if acc is None else acc + pv
    den = acc[:, LANES:LANES + 1]
    if sink is not None:
        den = den + jnp.exp2(sink - m)
    return acc[:, :LANES] * (1.0 / den)


def _pair_sink(sink_ref, h0, h1, t):
    row = lax.broadcasted_iota(jnp.int32, (2 * t, 1), 0)
    return jnp.where(row < t, sink_ref[h0], sink_ref[h1])


def _pair_heads(p):
    return p, p + A_Q_HEADS // A_KV_HEADS


def _even_attn_kernel(sink_ref, q_ref, k_ref, va_ref, vb_ref, ck_ref, cva_ref, cvb_ref, o_ref, *, seq, band):
    j = pl.program_id(1)
    tq = q_ref.shape[1]
    n_pairs = A_Q_HEADS * HEAD_DIM // LANES
    cka, ckb = ck_ref[0, :, 0:LANES], ck_ref[0, :, LANES:2 * LANES]
    cva, cvb = cva_ref[0], cvb_ref[0]
    ka, va = k_ref[0, :, 0:LANES], va_ref[0]
    for p in range(n_pairs):
        qs = _stack_pair(q_ref[0, :, p * LANES:(p + 1) * LANES])
        o = _softmax_pv([_nt_dot(qs, ka), _nt_dot(qs, cka)], [va, cva])
        o_ref[0, :, p * LANES:(p + 1) * LANES] = _unstack_pair(o).astype(o_ref.dtype)
    start = pl.multiple_of(jnp.clip((j - 1) * tq, 0, seq - band), Q_BLOCK)
    kb = k_ref[0, pl.ds(start, band), LANES:2 * LANES]
    vb = vb_ref[0, pl.ds(start, band), :]
    qpos = j * tq + lax.broadcasted_iota(jnp.int32, (tq, band), 0)
    kpos = start + lax.broadcasted_iota(jnp.int32, (tq, band), 1)
    ok = jnp.abs(kpos - qpos) <= WINDOW
    ok = jnp.concatenate([ok, ok], axis=0)
    off = n_pairs * LANES
    for p in range(n_pairs):
        h0, h1 = _pair_heads(p)
        qs = _stack_pair(q_ref[0, :, off + p * LANES:off + (p + 1) * LANES])
        s_win = jnp.where(ok, _nt_dot(qs, kb), NEG_INF)
        o = _softmax_pv([s_win, _nt_dot(qs, ckb)], [vb, cvb], _pair_sink(sink_ref, h0, h1, tq))
        o_ref[0, :, off + p * LANES:off + (p + 1) * LANES] = _unstack_pair(o).astype(o_ref.dtype)


def _even_attention(qkv, ckv, sink):
    b, s, _ = qkv.shape
    c = ckv.shape[1]
    tq = Q_BLOCK
    nq = 2 * A_Q_HEADS * HEAD_DIM
    w = 2 * LANES
    band = min(3 * Q_BLOCK, s)
    group = lambda rows, k: pl.BlockSpec((1, rows, w), lambda i, j: (i, 0, nq // w + k))
    return pl.pallas_call(
        functools.partial(_even_attn_kernel, seq=s, band=band),
        out_shape=jax.ShapeDtypeStruct((b, s, nq), BF16),
        grid=(b, s // tq),
        in_specs=[pl.BlockSpec(memory_space=pltpu.SMEM),
                  pl.BlockSpec((1, tq, nq), lambda i, j: (i, j, 0)),
                  group(s, 0), group(s, 1), group(s, 2), group(c, 0), group(c, 1), group(c, 2)],
        out_specs=pl.BlockSpec((1, tq, nq), lambda i, j: (i, j, 0)),
        compiler_params=_params("parallel", "parallel"),
        name="even_attention",
    )(sink, qkv, qkv, qkv, qkv, ckv, ckv, ckv)


def _even_ctx_attn_kernel(sink_ref, q_ref, ck_ref, cva_ref, cvb_ref, o_ref):
    tq = q_ref.shape[1]
    n_pairs = A_Q_HEADS * HEAD_DIM // LANES
    cka, ckb = ck_ref[0, :, 0:LANES], ck_ref[0, :, LANES:2 * LANES]
    cva, cvb = cva_ref[0], cvb_ref[0]
    off = n_pairs * LANES
    for p in range(n_pairs):
        qs = _stack_pair(q_ref[0, :, p * LANES:(p + 1) * LANES])
        o = _softmax_pv([_nt_dot(qs, cka)], [cva])
        o_ref[0, :, p * LANES:(p + 1) * LANES] = _unstack_pair(o).astype(o_ref.dtype)
        h0, h1 = _pair_heads(p)
        qs = _stack_pair(q_ref[0, :, off + p * LANES:off + (p + 1) * LANES])
        o = _softmax_pv([_nt_dot(qs, ckb)], [cvb], _pair_sink(sink_ref, h0, h1, tq))
        o_ref[0, :, off + p * LANES:off + (p + 1) * LANES] = _unstack_pair(o).astype(o_ref.dtype)


def _even_ctx_attention(ckv, sink):
    b, c, _ = ckv.shape
    nq = 2 * A_Q_HEADS * HEAD_DIM
    w = 2 * LANES
    group = lambda k: pl.BlockSpec((1, c, w), lambda i: (i, 0, nq // w + k))
    return pl.pallas_call(
        _even_ctx_attn_kernel,
        out_shape=jax.ShapeDtypeStruct((b, c, nq), BF16),
        grid=(b,),
        in_specs=[pl.BlockSpec(memory_space=pltpu.SMEM),
                  pl.BlockSpec((1, c, nq), lambda i: (i, 0, 0)),
                  group(0), group(1), group(2)],
        out_specs=pl.BlockSpec((1, c, nq), lambda i: (i, 0, 0)),
        compiler_params=_params("parallel"),
        name="even_ctx_attention",
    )(sink, ckv, ckv, ckv, ckv)


def _na_kernel(q_ref, k_ref, v_ref, ck_ref, cv_ref, bias_ref, o_ref, *, rows, kh):
    r = pl.program_id(1)
    rs = jnp.clip(r - kh // 2, 0, rows - kh)
    start = pl.multiple_of(rs * GRID_W, GRID_W)
    n_loc = kh * GRID_W
    for p in range(C_HEADS * HEAD_DIM // LANES):
        sl = slice(p * LANES, (p + 1) * LANES)
        vsl = slice(2 * p * LANES, 2 * (p + 1) * LANES)
        qs = _stack_pair(q_ref[0, :, sl])
        kb = k_ref[0, pl.ds(start, n_loc), sl]
        vb = v_ref[0, pl.ds(start, n_loc), vsl]
        s_loc = _nt_dot(qs, kb) + bias_ref[0, p]
        o = _softmax_pv([s_loc, _nt_dot(qs, ck_ref[0, :, sl])], [vb, cv_ref[0, :, vsl]])
        o_ref[0, :, sl] = _unstack_pair(o).astype(o_ref.dtype)


def _na_bias_table(rpb, rows):
    kh = min(NA_ROWS, rows)
    cols = np.arange(GRID_W)
    col_start = np.clip(cols - NA_COLS // 2, 0, GRID_W - NA_COLS)
    col_in = (cols[None, :] >= col_start[:, None]) & (cols[None, :] < col_start[:, None] + NA_COLS)
    pad = GRID_W - NA_COLS
    padded = jnp.pad(rpb.astype(F32), ((0, 0), (0, 0), (pad, pad)), mode="edge")
    t = jnp.stack([padded[:, :, GRID_W - 1 - c:2 * GRID_W - 1 - c] for c in range(GRID_W)], axis=2)
    t = jnp.where(col_in[None, None], t * LOG2E, NEG_INF)
    bands = []
    for d in range(kh):
        band = t[:, NA_ROWS - 1 - d:NA_ROWS - 1 - d + kh]
        bands.append(jnp.swapaxes(band, 1, 2).reshape(C_HEADS // 2, 2 * GRID_W, kh * GRID_W))
    return jnp.stack(bands, axis=0)


def _na_attention(qkv, ckv, bias, rows):
    b, s, _ = qkv.shape
    c = ckv.shape[1]
    hd = C_HEADS * HEAD_DIM
    kh = min(NA_ROWS, rows)

    def bias_map(i, r):
        return (r - jnp.clip(r - kh // 2, 0, rows - kh), 0, 0, 0)

    return pl.pallas_call(
        functools.partial(_na_kernel, rows=rows, kh=kh),
        out_shape=jax.ShapeDtypeStruct((b, s, hd), BF16),
        grid=(b, rows),
        in_specs=[pl.BlockSpec((1, GRID_W, hd), lambda i, r: (i, r, 0)),
                  pl.BlockSpec((1, s, hd), lambda i, r: (i, 0, 1)),
                  pl.BlockSpec((1, s, 2 * hd), lambda i, r: (i, 0, 1)),
                  pl.BlockSpec((1, c, hd), lambda i, r: (i, 0, 2)),
                  pl.BlockSpec((1, c, 2 * hd), lambda i, r: (i, 0, 0)),
                  pl.BlockSpec((1,) + bias.shape[1:], bias_map)],
        out_specs=pl.BlockSpec((1, GRID_W, hd), lambda i, r: (i, r, 0)),
        compiler_params=_params("parallel", "parallel"),
        name="na_attention",
    )(qkv, qkv, qkv, ckv, ckv, bias)


def _outproj_core(o_ref, w_ref, x_ref, m_ref, gpost_ref, gpre_ref):
    y = _dot(o_ref[0], w_ref[...])
    x = x_ref[0] + m_ref[0, 2:3, :] * (_rms(y) * gpost_ref[...])
    h = (_rms(x) * gpre_ref[...]) * (1.0 + m_ref[0, 4:5, :]) + m_ref[0, 3:4, :]
    return x, h


def _outproj_kernel(o_ref, w_ref, x_ref, m_ref, gpost_ref, gpre_ref, xo_ref, h_ref):
    x, h = _outproj_core(o_ref, w_ref, x_ref, m_ref, gpost_ref, gpre_ref)
    xo_ref[0] = x
    h_ref[0] = h.astype(h_ref.dtype)


def _split_bf16(a):
    hi = a.astype(BF16)
    return hi, (a - hi.astype(F32)).astype(BF16)


def _outproj_router_kernel(o_ref, w_ref, x_ref, m_ref, gpost_ref, gpre_ref, wr_hi_ref, wr_lo_ref,
                           xo_ref, h_ref, rank_ref, gate_ref, rank_t_ref, cnt_ref):
    x, h = _outproj_core(o_ref, w_ref, x_ref, m_ref, gpost_ref, gpre_ref)
    xo_ref[0] = x
    h_ref[0] = h.astype(h_ref.dtype)
    h_hi, h_lo = _split_bf16(h)
    logits = _dot(h_hi, wr_hi_ref[...]) + (_dot(h_hi, wr_lo_ref[...]) + _dot(h_lo, wr_hi_ref[...]))
    t = logits.shape[0]
    lane = lax.broadcasted_iota(jnp.int32, logits.shape, 1).astype(F32)
    logits = jnp.where(lane < N_EXPERTS, logits, -jnp.inf)
    v1 = jnp.max(logits, axis=-1, keepdims=True)
    i1 = jnp.min(jnp.where(logits == v1, lane, float(LANES)), axis=-1, keepdims=True)
    rest = jnp.where(lane == i1, -jnp.inf, logits)
    v2 = jnp.max(rest, axis=-1, keepdims=True)
    i2 = jnp.min(jnp.where(rest == v2, lane, float(LANES)), axis=-1, keepdims=True)
    e2 = jnp.exp(v2 - v1)
    w1 = 1.0 / (1.0 + e2)
    w2 = e2 / (1.0 + e2)
    sel1, sel2 = lane == i1, lane == i2
    sel = (sel1 | sel2).astype(F32)
    rr = lax.broadcasted_iota(jnp.int32, (t, t), 0)
    cc = lax.broadcasted_iota(jnp.int32, (t, t), 1)
    before = _dot(jnp.where(rr > cc, 1.0, 0.0).astype(BF16), sel.astype(BF16))
    rank = jnp.where(sel > 0.0, before, -1.0)
    rank_ref[0] = rank[:, :N_EXPERTS]
    gate_ref[0] = jnp.where(sel1, w1, jnp.where(sel2, w2, 0.0))[:, :N_EXPERTS]
    rank_t_ref[0] = rank.T[:N_EXPERTS, :]
    cnt_ref[0] = jnp.sum(sel, axis=0, keepdims=True)


def _outproj(o, w, x, mods, mod_off, g_post, g_pre, *, router=None, name):
    grp, n, d = x.shape
    dm = o.shape[2]
    tm = _tile(n, 512)
    tok = lambda b, t: (b, t, 0)
    const = lambda b, t: (0, 0)
    in_specs = [pl.BlockSpec((1, tm, dm), tok),
                pl.BlockSpec((dm, d), const),
                pl.BlockSpec((1, tm, d), tok),
                pl.BlockSpec((1, N_MOD, d), lambda b, t: (b + mod_off, 0, 0)),
                pl.BlockSpec((1, d), const),
                pl.BlockSpec((1, d), const)]
    args = [o, w, x, mods, g_post.reshape(1, d), g_pre.reshape(1, d)]
    if router is None:
        return pl.pallas_call(
            _outproj_kernel,
            out_shape=(jax.ShapeDtypeStruct((grp, n, d), F32), jax.ShapeDtypeStruct((grp, n, d), BF16)),
            grid=(grp, n // tm),
            in_specs=in_specs,
            out_specs=(pl.BlockSpec((1, tm, d), tok), pl.BlockSpec((1, tm, d), tok)),
            compiler_params=_params("parallel", "parallel"),
            name=name,
        )(*args)
    wr_hi, wr_lo = router
    in_specs += [pl.BlockSpec((d, LANES), const), pl.BlockSpec((d, LANES), const)]
    args += [wr_hi, wr_lo]
    per = n // tm
    blk = lambda b, t: (b * per + t, 0, 0)
    ne = N_EXPERTS
    return pl.pallas_call(
        _outproj_router_kernel,
        out_shape=(jax.ShapeDtypeStruct((grp, n, d), F32), jax.ShapeDtypeStruct((grp, n, d), BF16),
                   jax.ShapeDtypeStruct((grp, n, ne), F32), jax.ShapeDtypeStruct((grp, n, ne), F32),
                   jax.ShapeDtypeStruct((grp * per, ne, tm), F32), jax.ShapeDtypeStruct((grp * per, 1, LANES), F32)),
        grid=(grp, per),
        in_specs=in_specs,
        out_specs=(pl.BlockSpec((1, tm, d), tok), pl.BlockSpec((1, tm, d), tok),
                   pl.BlockSpec((1, tm, ne), tok), pl.BlockSpec((1, tm, ne), tok),
                   pl.BlockSpec((1, ne, tm), blk), pl.BlockSpec((1, 1, LANES), blk)),
        compiler_params=_params("parallel", "parallel"),
        name=name,
    )(*args)


def _swiglu_chunk(h, wg, wu, wd):
    g = _dot(h, wg)
    u = _dot(h, wu)
    a = (g / (1.0 + jnp.exp(-g))) * u
    return _dot(a.astype(BF16), wd)


SUB_ROWS = 256


def _accumulate_swiglu(c, rows_of, acc_ref, wg, wu, wd, n_rows):
    sub = min(SUB_ROWS, n_rows)
    for r in range(n_rows // sub):
        rows = slice(r * sub, (r + 1) * sub)
        part = _swiglu_chunk(rows_of(rows), wg, wu, wd)
        acc_ref[rows] = jnp.where(c > 0, acc_ref[rows], 0.0) + part


def _ffn_kernel(h_ref, wg_ref, wu_ref, wd_ref, x_ref, m_ref, g_ref, o_ref, acc_ref):
    c = pl.program_id(2)
    _accumulate_swiglu(c, lambda rows: h_ref[0, rows], acc_ref, wg_ref[...], wu_ref[...], wd_ref[...],
                       acc_ref.shape[0])

    @pl.when(c == pl.num_programs(2) - 1)
    def _():
        o_ref[0] = x_ref[0] + m_ref[0, 5:6, :] * (_rms(acc_ref[...]) * g_ref[...])


def _ffn(h, wg, wu, wd, x, mods, mod_off, g_post, *, name):
    grp, n, d = x.shape
    ff = wg.shape[1]
    tm = _tile(n, 1024)
    fc = _tile(ff, 1408)
    tok = lambda b, t, c: (b, t, 0)
    return pl.pallas_call(
        _ffn_kernel,
        out_shape=jax.ShapeDtypeStruct((grp, n, d), F32),
        grid=(grp, n // tm, ff // fc),
        in_specs=[pl.BlockSpec((1, tm, d), tok),
                  pl.BlockSpec((d, fc), lambda b, t, c: (0, c)),
                  pl.BlockSpec((d, fc), lambda b, t, c: (0, c)),
                  pl.BlockSpec((fc, d), lambda b, t, c: (c, 0)),
                  pl.BlockSpec((1, tm, d), tok),
                  pl.BlockSpec((1, N_MOD, d), lambda b, t, c: (b + mod_off, 0, 0)),
                  pl.BlockSpec((1, d), lambda b, t, c: (0, 0))],
        out_specs=pl.BlockSpec((1, tm, d), tok),
        scratch_shapes=[pltpu.VMEM((tm, d), F32)],
        compiler_params=_params("parallel", "parallel", "arbitrary"),
        name=name,
    )(h, wg, wu, wd, x, mods, g_post.reshape(1, d))


ROW_ALIGN = 8


def _for_pieces(n, max_piece, fn):
    off = 0
    size = max_piece
    while size >= ROW_ALIGN:
        bit = (n & size) != 0

        @pl.when(bit)
        def _(off=off, size=size):
            fn(pl.multiple_of(off, ROW_ALIGN), size)

        off = off + jnp.where(bit, size, 0)
        size //= 2


def _dispatch_kernel(start_ref, n_ref, zstart_ref, zn_ref, h_ref, rank_t_ref, xs_ref, stage_ref, zero_ref, sem):
    b = pl.program_id(0)
    tm = h_ref.shape[0]
    half = tm // 2
    h = h_ref[...]

    def out_copy(e, start, off, size):
        dst = xs_ref.at[pl.ds(pl.multiple_of(start + off, ROW_ALIGN), size)]
        return pltpu.make_async_copy(stage_ref.at[e, pl.ds(off, size)], dst, sem.at[e])

    def zero_copy(start, off, size):
        dst = xs_ref.at[pl.ds(pl.multiple_of(start + off, ROW_ALIGN), size)]
        return pltpu.make_async_copy(zero_ref.at[pl.ds(0, size)], dst, sem.at[N_EXPERTS])

    @pl.when(b == 0)
    def _():
        zero_ref[...] = jnp.zeros_like(zero_ref)
        dead_from = zn_ref[N_EXPERTS]
        n_slabs = xs_ref.shape[0] // tm

        def slab(i):
            return zero_copy(pl.multiple_of(i * tm, tm), 0, tm)

        for k in range(N_EXPERTS):
            _for_pieces(zn_ref[k], tm, lambda off, size, k=k: zero_copy(zstart_ref[k], off, size).start())
        lax.fori_loop(dead_from, n_slabs, lambda i, carry: (slab(i).start(), carry)[1], 0)
        for k in range(N_EXPERTS):
            _for_pieces(zn_ref[k], tm, lambda off, size, k=k: zero_copy(zstart_ref[k], off, size).wait())
        lax.fori_loop(dead_from, n_slabs, lambda i, carry: (slab(i).wait(), carry)[1], 0)

    for e in range(N_EXPERTS):
        n = n_ref[b * N_EXPERTS + e]
        rank_row = rank_t_ref[0, e:e + 1, :]
        for hf in range(2):
            @pl.when(n > hf * half)
            def _(hf=hf, e=e, rank_row=rank_row):
                r = lax.broadcasted_iota(jnp.int32, (half, tm), 0).astype(F32) + float(hf * half)
                onehot = jnp.where(r == rank_row, 1.0, 0.0).astype(BF16)
                stage_ref[e, hf * half:(hf + 1) * half, :] = _dot(onehot, h)
        _for_pieces(n, tm, lambda off, size, e=e: out_copy(e, start_ref[b * N_EXPERTS + e], off, size).start())
    for e in range(N_EXPERTS):
        n = n_ref[b * N_EXPERTS + e]
        _for_pieces(n, tm, lambda off, size, e=e: out_copy(e, start_ref[b * N_EXPERTS + e], off, size).wait())


def _dispatch(h, rank_t, starts, counts, zstarts, zcounts, n_rows):
    n, d = h.shape
    nb, _, tm = rank_t.shape
    return pl.pallas_call(
        _dispatch_kernel,
        out_shape=jax.ShapeDtypeStruct((n_rows, d), F32),
        grid_spec=pltpu.PrefetchScalarGridSpec(
            num_scalar_prefetch=4,
            grid=(nb,),
            in_specs=[pl.BlockSpec((tm, d), lambda i, *_: (i, 0)),
                      pl.BlockSpec((1, N_EXPERTS, tm), lambda i, *_: (i, 0, 0))],
            out_specs=pl.BlockSpec(memory_space=pl.ANY),
            scratch_shapes=[pltpu.VMEM((N_EXPERTS, tm, d), F32), pltpu.VMEM((tm, d), F32),
                            pltpu.SemaphoreType.DMA((N_EXPERTS + 1,))]),
        compiler_params=_params("arbitrary"),
        name="moe_dispatch",
    )(starts, counts, zstarts, zcounts, h, rank_t)


def _experts_kernel(te_ref, nt_ref, xs_ref, wg_ref, wu_ref, wd_ref, ys_ref):
    i = pl.program_id(0)
    c = pl.program_id(1)

    @pl.when(i < nt_ref[0])
    def _():
        _accumulate_swiglu(c, lambda rows: xs_ref[rows].astype(BF16), ys_ref, wg_ref[0], wu_ref[0], wd_ref[0],
                           ys_ref.shape[0])

    @pl.when((i >= nt_ref[0]) & (c == 0))
    def _():
        ys_ref[...] = jnp.zeros_like(ys_ref)


def _experts(xs, tile_expert, n_tiles, wg, wu, wd, tmg):
    n_rows, d = xs.shape
    ff = wg.shape[2]
    fc = _tile(ff, 1792)
    total = n_rows // tmg

    def row_map(i, c, te, nt):
        return (jnp.minimum(i, nt[0] - 1), 0)

    def w_in_map(i, c, te, nt):
        return (te[jnp.minimum(i, nt[0] - 1)], 0, jnp.where(i < nt[0], c, ff // fc - 1))

    def w_out_map(i, c, te, nt):
        return (te[jnp.minimum(i, nt[0] - 1)], jnp.where(i < nt[0], c, ff // fc - 1), 0)

    return pl.pallas_call(
        _experts_kernel,
        out_shape=jax.ShapeDtypeStruct((n_rows, d), F32),
        grid_spec=pltpu.PrefetchScalarGridSpec(
            num_scalar_prefetch=2,
            grid=(total, ff // fc),
            in_specs=[pl.BlockSpec((tmg, d), row_map),
                      pl.BlockSpec((1, d, fc), w_in_map),
                      pl.BlockSpec((1, d, fc), w_in_map),
                      pl.BlockSpec((1, fc, d), w_out_map)],
            out_specs=pl.BlockSpec((tmg, d), lambda i, c, te, nt: (i, 0))),
        compiler_params=_params("arbitrary", "arbitrary"),
        name="moe_experts",
    )(tile_expert, n_tiles, xs, wg, wu, wd)


def _combine_kernel(start_ref, n_ref, ys_ref, rank_ref, gate_ref, x_ref, m_ref, g_ref, o_ref, buf_ref, acc_ref, sem):
    per = pl.num_programs(1)
    b = pl.program_id(0) * per + pl.program_id(1)
    tm = x_ref.shape[1]
    half = tm // 2

    def in_copy(e, off, size):
        start = start_ref[b * N_EXPERTS + e]
        src = ys_ref.at[pl.ds(pl.multiple_of(start + off, ROW_ALIGN), size)]
        return pltpu.make_async_copy(src, buf_ref.at[e, pl.ds(off, size)], sem.at[e])

    @pl.when(b == 0)
    def _():
        buf_ref[...] = jnp.zeros_like(buf_ref)

    for e in range(N_EXPERTS):
        _for_pieces(n_ref[b * N_EXPERTS + e], tm, lambda off, size, e=e: in_copy(e, off, size).start())
    acc_ref[...] = jnp.zeros_like(acc_ref)
    rank = rank_ref[0]
    gate = gate_ref[0]
    for e in range(N_EXPERTS):
        n = n_ref[b * N_EXPERTS + e]
        _for_pieces(n, tm, lambda off, size, e=e: in_copy(e, off, size).wait())
        for hf in range(2):
            @pl.when(n > hf * half)
            def _(hf=hf, e=e):
                col = lax.broadcasted_iota(jnp.int32, (tm, half), 1).astype(F32) + float(hf * half)
                onehot = jnp.where(rank[:, e:e + 1] == col, 1.0, 0.0).astype(BF16)
                y = buf_ref[e, hf * half:(hf + 1) * half, :].astype(BF16)
                acc_ref[...] += gate[:, e:e + 1] * _dot(onehot, y)
    o_ref[0] = x_ref[0] + m_ref[0, 5:6, :] * (_rms(acc_ref[...]) * g_ref[...])


def _combine(ys, starts, counts, rank, gate, x, mods, g_post, tm):
    b, s, d = x.shape
    per = s // tm
    tok = lambda i, t, *_: (i, t, 0)
    return pl.pallas_call(
        _combine_kernel,
        out_shape=jax.ShapeDtypeStruct((b, s, d), F32),
        grid_spec=pltpu.PrefetchScalarGridSpec(
            num_scalar_prefetch=2,
            grid=(b, per),
            in_specs=[pl.BlockSpec(memory_space=pl.ANY),
                      pl.BlockSpec((1, tm, N_EXPERTS), tok),
                      pl.BlockSpec((1, tm, N_EXPERTS), tok),
                      pl.BlockSpec((1, tm, d), tok),
                      pl.BlockSpec((1, N_MOD, d), lambda i, t, *_: (i, 0, 0)),
                      pl.BlockSpec((1, d), lambda i, t, *_: (0, 0))],
            out_specs=pl.BlockSpec((1, tm, d), tok),
            scratch_shapes=[pltpu.VMEM((N_EXPERTS, tm, d), F32), pltpu.VMEM((tm, d), F32),
                            pltpu.SemaphoreType.DMA((N_EXPERTS,))]),
        compiler_params=_params("arbitrary", "arbitrary"),
        name="moe_combine",
    )(starts, counts, ys, rank, gate, x, mods, g_post.reshape(1, d))


def _moe(h, rank, gate, rank_t, cnt, x, mods, g_post, wg, wu, wd):
    b, s, d = x.shape
    n = b * s
    nb, _, tm = rank_t.shape
    tmg = _tile(2 * n, 1024)
    total_tiles = -(-(2 * n + (ROW_ALIGN - 1) * N_EXPERTS * nb) // tmg) + N_EXPERTS
    n_rows = total_tiles * tmg
    per_block = cnt[:, 0, :N_EXPERTS].astype(jnp.int32)
    chunk = (per_block + ROW_ALIGN - 1) // ROW_ALIGN * ROW_ALIGN
    rows_e = jnp.sum(chunk, axis=0)
    tiles_per = (rows_e + tmg - 1) // tmg
    tile_end = jnp.cumsum(tiles_per)
    offsets = (tile_end - tiles_per) * tmg
    starts = (offsets[None, :] + jnp.cumsum(chunk, axis=0) - chunk).reshape(nb * N_EXPERTS)
    counts = chunk.reshape(nb * N_EXPERTS)
    zstarts = offsets + rows_e
    zcounts = jnp.concatenate([tiles_per * tmg - rows_e, tile_end[-1:] * (tmg // tm)])
    tile_expert = jnp.sum((tile_end[None, :] <= jnp.arange(total_tiles)[:, None]).astype(jnp.int32), axis=1)
    tile_expert = jnp.minimum(tile_expert, N_EXPERTS - 1)
    n_tiles = tile_end[-1:].astype(jnp.int32)
    xs = _dispatch(h.reshape(n, d), rank_t, starts, counts, zstarts, zcounts.astype(jnp.int32), n_rows)
    ys = _experts(xs, tile_expert, n_tiles, wg, wu, wd, tmg)
    return _combine(ys, starts, counts, rank, gate, x, mods, g_post, tm)


def _even_layouts(w_in, w_out, g_q, g_k, sink):
    dh, half = HEAD_DIM, HEAD_DIM // 2
    within = np.concatenate([np.arange(0, dh, 2), np.arange(1, dh, 2)])
    g = A_Q_HEADS // A_KV_HEADS
    q_order = np.array([h for p in range(g) for h in _pair_heads(p)])
    sizes = [A_Q_HEADS * dh, A_KV_HEADS * dh, A_KV_HEADS * dh, B_Q_HEADS * dh, B_KV_HEADS * dh, B_KV_HEADS * dh]
    base = np.concatenate([[0], np.cumsum(sizes)])

    def head_cols(start, order, permute):
        inner = within if permute else np.arange(dh)
        return np.concatenate([start + h * dh + inner for h in order])

    kv_order = np.arange(A_KV_HEADS)
    cols = np.concatenate([head_cols(base[0], q_order, True), head_cols(base[3], q_order, True),
                           head_cols(base[1], kv_order, True), head_cols(base[4], kv_order, True),
                           head_cols(base[2], kv_order, False), head_cols(base[5], kv_order, False)])
    rows_out = np.concatenate([head_cols(0, q_order, False), head_cols(A_Q_HEADS * dh, q_order, False)])
    gq = jnp.tile(g_q[within], LANES // dh).reshape(1, LANES)
    gk = jnp.tile(g_k[within], LANES // dh).reshape(1, LANES)
    lane = np.arange(LANES)
    head_mean = jnp.asarray((lane[:, None] // dh == lane[None, :] // dh) / dh, BF16)
    return (w_in[:, cols].astype(BF16), w_out[rows_out].astype(BF16), gq, gk, head_mean,
            sink.astype(F32) * LOG2E)


def _rope_tables(n_tokens):
    t = jnp.arange(n_tokens, dtype=jnp.int32)
    row = (t // GRID_W).astype(F32)
    col = (t % GRID_W).astype(F32)
    half = HEAD_DIM // 2
    freqs = ROPE_THETA ** (-jnp.arange(0, half, 2, dtype=F32) / half)
    ang = jnp.concatenate([row[:, None] * freqs, col[:, None] * freqs], axis=-1)
    cos, sin = jnp.cos(ang), jnp.sin(ang)
    reps = LANES // HEAD_DIM
    return (jnp.tile(jnp.concatenate([cos, cos], axis=-1), (1, reps)),
            jnp.tile(jnp.concatenate([-sin, sin], axis=-1), (1, reps)))


def kernel(x, c, ctx, c_ctx, e_w_mod, e_b_mod, e_g_pre_mix, e_g_post_mix, e_g_pre_ffn, e_g_post_ffn, e_w_in, e_w_out, e_g_q, e_g_k, e_sink, e_w_gate, e_w_up, e_w_down, o_w_mod, o_b_mod, o_g_pre_mix, o_g_post_mix, o_g_pre_ffn, o_g_post_ffn, o_w_in, o_w_out, o_rpb, o_w_router, o_w_gate, o_w_up, o_w_down):
    b, s, d = x.shape
    n_ctx = ctx.shape[1]
    rows = s // GRID_W
    ctx = ctx.reshape(1, b * n_ctx, d)

    mods = _mod_vectors(c, c_ctx, e_w_mod[0], e_b_mod[0])
    w_in, w_out, gq, gk, head_mean, sink = _even_layouts(e_w_in[0], e_w_out[0], e_g_q[0], e_g_k[0], e_sink[0])
    even = (gq, gk, head_mean)
    qkv = _inproj(x, mods, 0, e_g_pre_mix[0], w_in, even=even, rope_tabs=_rope_tables(s), name="inproj0_lat")
    ckv = _inproj(ctx, mods, b, e_g_pre_mix[0], w_in, even=even, rope_tabs=(None, None), name="inproj0_ctx")
    ckv = ckv.reshape(b, n_ctx, -1)
    o_lat = _even_attention(qkv, ckv, sink)
    o_ctx = _even_ctx_attention(ckv, sink).reshape(1, b * n_ctx, -1)
    wg, wu, wd = e_w_gate[0].astype(BF16), e_w_up[0].astype(BF16), e_w_down[0].astype(BF16)
    x, h = _outproj(o_lat, w_out, x, mods, 0, e_g_post_mix[0], e_g_pre_ffn[0], name="outproj0_lat")
    x = _ffn(h, wg, wu, wd, x, mods, 0, e_g_post_ffn[0], name="ffn0_lat")
    ctx, h = _outproj(o_ctx, w_out, ctx, mods, b, e_g_post_mix[0], e_g_pre_ffn[0], name="outproj0_ctx")
    ctx = _ffn(h, wg, wu, wd, ctx, mods, b, e_g_post_ffn[0], name="ffn0_ctx")

    mods = _mod_vectors(c, c_ctx, o_w_mod[0], o_b_mod[0])
    hd = C_HEADS * HEAD_DIM
    w_in = o_w_in[0].astype(BF16)
    qkv = _inproj(x, mods, 0, o_g_pre_mix[0], w_in, segments=(("q", hd), ("k", hd), ("v", hd)), name="inproj1_lat")
    w_vk = jnp.concatenate([w_in[:, 2 * hd:], w_in[:, hd:2 * hd]], axis=1)
    ckv = _inproj(ctx, mods, b, o_g_pre_mix[0], w_vk, segments=(("v", hd), ("k", hd)), name="inproj1_ctx")
    ckv = ckv.reshape(b, n_ctx, 3 * hd)
    o_lat = _na_attention(qkv, ckv, _na_bias_table(o_rpb[0], rows), rows)
    wr = jnp.zeros((d, LANES), F32).at[:, :N_EXPERTS].set(o_w_router[0])
    x, h, rank, gate, rank_t, cnt = _outproj(o_lat, o_w_out[0].astype(BF16), x, mods, 0, o_g_post_mix[0],
                                             o_g_pre_ffn[0], router=_split_bf16(wr), name="outproj1_router")
    return _moe(h, rank, gate, rank_t, cnt, x, mods, o_g_post_ffn[0],
                o_w_gate[0].astype(BF16), o_w_up[0].astype(BF16), o_w_down[0].astype(BF16))
```

```python
import functools

import numpy as np
import jax
import jax.numpy as jnp
from jax import lax
from jax.experimental import pallas as pl
from jax.experimental.pallas import tpu as pltpu

GRID_W = 64
HEAD_DIM = 64
A_Q_HEADS = 8
A_KV_HEADS = 2
B_Q_HEADS = 8
B_KV_HEADS = 2
C_HEADS = 16
Q_BLOCK = 128
WINDOW = 128
NA_ROWS = 8
NA_COLS = 16
ROPE_THETA = 10000.0
N_EXPERTS = 8
N_MOD = 6
EPS = 1e-6
NEG_INF = -1e30
ATTN_SCALE = HEAD_DIM ** -0.5
LOG2E = 1.4426950408889634
Q_SCALE = ATTN_SCALE * LOG2E

LANES = 128
VMEM_LIMIT = 56 * 1024 * 1024

F32 = jnp.float32
BF16 = jnp.bfloat16
HIGHEST = lax.Precision.HIGHEST


def _tile(n, pref):
    return pref if n % pref == 0 else n


def _params(*sem):
    return pltpu.CompilerParams(dimension_semantics=sem, vmem_limit_bytes=VMEM_LIMIT)


def _rms(x):
    return x * lax.rsqrt(jnp.mean(x * x, axis=-1, keepdims=True) + EPS)


def _nt_dot(a, b):
    return lax.dot_general(a, b, (((1,), (1,)), ((), ())), preferred_element_type=F32)


def _dot(a, b):
    return jnp.dot(a, b, preferred_element_type=F32)


def _mod_kernel(c_ref, w_ref, b_ref, o_ref):
    c = c_ref[...]
    a = c / (1.0 + jnp.exp(-c))
    o_ref[...] = jnp.dot(a, w_ref[...], precision=HIGHEST, preferred_element_type=F32) + b_ref[...]


def _mod_vectors(c, c_ctx, w_mod, b_mod):
    b, d = c.shape
    rows = -(-(b + 1) // 8) * 8
    cc = jnp.zeros((rows, d), F32).at[:b].set(c).at[b].set(c_ctx)
    n = w_mod.shape[1]
    tn = _tile(n, 1536)
    m = pl.pallas_call(
        _mod_kernel,
        out_shape=jax.ShapeDtypeStruct((rows, n), F32),
        grid=(n // tn,),
        in_specs=[pl.BlockSpec((rows, d), lambda j: (0, 0)),
                  pl.BlockSpec((d, tn), lambda j: (0, j)),
                  pl.BlockSpec((1, tn), lambda j: (0, j))],
        out_specs=pl.BlockSpec((rows, tn), lambda j: (0, j)),
        compiler_params=_params("arbitrary"),
        name="mod_vectors",
    )(cc, w_mod, b_mod.reshape(1, n))
    return m.reshape(rows, N_MOD, d)


def _modulated(x_ref, m_ref, g_ref):
    x = x_ref[0]
    return (_rms(x) * g_ref[...]) * (1.0 + m_ref[0, 1:2, :]) + m_ref[0, 0:1, :]


def _rope_block(y, cos, sin_signed):
    lane = lax.broadcasted_iota(jnp.int32, y.shape, 1)
    partner = jnp.where(lane % HEAD_DIM < HEAD_DIM // 2,
                        pltpu.roll(y, LANES - HEAD_DIM // 2, 1),
                        pltpu.roll(y, HEAD_DIM // 2, 1))
    return y * cos + partner * sin_signed


def _store_value_block(o_ref, col, vb):
    o_ref[0, :, col:col + LANES] = vb.astype(o_ref.dtype)
    o_ref[0, :, col + LANES:col + 2 * LANES] = jnp.ones(vb.shape, o_ref.dtype)


def _inproj_even_kernel(x_ref, m_ref, g_ref, w_ref, gq_ref, gk_ref, hm_ref, cos_ref, sin_ref, o_ref, *, rope):
    h = _modulated(x_ref, m_ref, g_ref).astype(BF16)
    y = _dot(h, w_ref[...])
    n_qa = A_Q_HEADS * HEAD_DIM // LANES
    n_q = n_qa + B_Q_HEADS * HEAD_DIM // LANES
    for i in range(y.shape[1] // LANES):
        yb = y[:, i * LANES:(i + 1) * LANES]
        is_qa, is_q, is_ka, is_k = i < n_qa, i < n_q, i == n_q, n_q <= i < n_q + 2
        if is_qa or is_ka:
            ms = _dot((yb * yb).astype(BF16), hm_ref[...])
            yb = yb * lax.rsqrt(ms + EPS) * (gq_ref[...] if is_qa else gk_ref[...])
        if rope and (is_q or is_k):
            yb = _rope_block(yb, cos_ref[...], sin_ref[...])
        if is_q:
            yb = yb * Q_SCALE
        if is_q or is_k:
            o_ref[0, :, i * LANES:(i + 1) * LANES] = yb.astype(o_ref.dtype)
        else:
            _store_value_block(o_ref, (n_q + 2 + 2 * (i - n_q - 2)) * LANES, yb)


def _inproj_plain_kernel(x_ref, m_ref, g_ref, w_ref, o_ref, *, segments):
    h = _modulated(x_ref, m_ref, g_ref).astype(BF16)
    y = _dot(h, w_ref[...])
    src = dst = 0
    for kind, width in segments:
        seg = y[:, src:src + width]
        if kind == "v":
            for p in range(width // LANES):
                _store_value_block(o_ref, dst + 2 * p * LANES, seg[:, p * LANES:(p + 1) * LANES])
            dst += 2 * width
        else:
            o_ref[0, :, dst:dst + width] = (seg * Q_SCALE if kind == "q" else seg).astype(o_ref.dtype)
            dst += width
        src += width


def _inproj(x, mods, mod_off, g, w, *, even=None, rope_tabs=None, segments=None, name):
    grp, n, d = x.shape
    if even is not None:
        nout = w.shape[1] + (A_KV_HEADS + B_KV_HEADS) * HEAD_DIM
    else:
        nout = sum(width * (2 if kind == "v" else 1) for kind, width in segments)
    tm = _tile(n, 512)
    grid = (grp, n // tm)
    in_specs = [pl.BlockSpec((1, tm, d), lambda b, t: (b, t, 0)),
                pl.BlockSpec((1, N_MOD, d), lambda b, t: (b + mod_off, 0, 0)),
                pl.BlockSpec((1, d), lambda b, t: (0, 0)),
                pl.BlockSpec(w.shape, lambda b, t: (0, 0))]
    args = [x, mods, g.reshape(1, d), w]
    if even is not None:
        gq, gk, hm = even
        cos, sin = rope_tabs
        rope = cos is not None
        if not rope:
            cos = jnp.zeros((tm, LANES), F32)
            sin = cos
            tab_map = lambda b, t: (0, 0)
        else:
            tab_map = lambda b, t: (t, 0)
        in_specs += [pl.BlockSpec((1, LANES), lambda b, t: (0, 0)),
                     pl.BlockSpec((1, LANES), lambda b, t: (0, 0)),
                     pl.BlockSpec((LANES, LANES), lambda b, t: (0, 0)),
                     pl.BlockSpec((tm, LANES), tab_map),
                     pl.BlockSpec((tm, LANES), tab_map)]
        args += [gq, gk, hm, cos, sin]
        body = functools.partial(_inproj_even_kernel, rope=rope)
    else:
        body = functools.partial(_inproj_plain_kernel, segments=segments)
    return pl.pallas_call(
        body,
        out_shape=jax.ShapeDtypeStruct((grp, n, nout), BF16),
        grid=grid,
        in_specs=in_specs,
        out_specs=pl.BlockSpec((1, tm, nout), lambda b, t: (b, t, 0)),
        compiler_params=_params("parallel", "parallel"),
        name=name,
    )(*args)


def _stack_pair(qp):
    lane = lax.broadcasted_iota(jnp.int32, qp.shape, 1)
    zero = jnp.zeros_like(qp)
    return jnp.concatenate([jnp.where(lane < HEAD_DIM, qp, zero),
                            jnp.where(lane >= HEAD_DIM, qp, zero)], axis=0)


def _unstack_pair(o):
    t = o.shape[0] // 2
    lane = lax.broadcasted_iota(jnp.int32, (t, LANES), 1)
    return jnp.where(lane < HEAD_DIM, o[:t], o[t:])


def _softmax_pv(scores, values, sink=None):
    m = functools.reduce(jnp.maximum, [jnp.max(s, axis=-1, keepdims=True) for s in scores])
    if sink is not None:
        m = jnp.maximum(m, sink)
    acc = None
    for s, v in zip(scores, values):
        pv = _dot(jnp.exp2(s - m).astype(v.dtype), v)
        acc = pv if acc is None else acc + pv
    den = acc[:, LANES:]
    if sink is not None:
        den = den + jnp.exp2(sink - m)
    return acc[:, :LANES] * (1.0 / den)


def _pair_sink(sink_ref, h0, h1, t):
    row = lax.broadcasted_iota(jnp.int32, (2 * t, 1), 0)
    return jnp.where(row < t, sink_ref[h0], sink_ref[h1])


def _pair_heads(p):
    return p, p + A_Q_HEADS // A_KV_HEADS


def _even_attn_kernel(sink_ref, q_ref, k_ref, va_ref, vb_ref, ck_ref, cva_ref, cvb_ref, o_ref, *, seq, band):
    j = pl.program_id(1)
    tq = q_ref.shape[1]
    n_pairs = A_Q_HEADS * HEAD_DIM // LANES
    cka, ckb = ck_ref[0, :, 0:LANES], ck_ref[0, :, LANES:2 * LANES]
    cva, cvb = cva_ref[0], cvb_ref[0]
    ka, va = k_ref[0, :, 0:LANES], va_ref[0]
    for p in range(n_pairs):
        qs = _stack_pair(q_ref[0, :, p * LANES:(p + 1) * LANES])
        o = _softmax_pv([_nt_dot(qs, ka), _nt_dot(qs, cka)], [va, cva])
        o_ref[0, :, p * LANES:(p + 1) * LANES] = _unstack_pair(o).astype(o_ref.dtype)
    start = pl.multiple_of(jnp.clip((j - 1) * tq, 0, seq - band), Q_BLOCK)
    kb = k_ref[0, pl.ds(start, band), LANES:2 * LANES]
    vb = vb_ref[0, pl.ds(start, band), :]
    qpos = j * tq + lax.broadcasted_iota(jnp.int32, (tq, band), 0)
    kpos = start + lax.broadcasted_iota(jnp.int32, (tq, band), 1)
    ok = jnp.abs(kpos - qpos) <= WINDOW
    ok = jnp.concatenate([ok, ok], axis=0)
    off = n_pairs * LANES
    for p in range(n_pairs):
        h0, h1 = _pair_heads(p)
        qs = _stack_pair(q_ref[0, :, off + p * LANES:off + (p + 1) * LANES])
        s_win = jnp.where(ok, _nt_dot(qs, kb), NEG_INF)
        o = _softmax_pv([s_win, _nt_dot(qs, ckb)], [vb, cvb], _pair_sink(sink_ref, h0, h1, tq))
        o_ref[0, :, off + p * LANES:off + (p + 1) * LANES] = _unstack_pair(o).astype(o_ref.dtype)


def _even_attention(qkv, ckv, sink):
    b, s, _ = qkv.shape
    c = ckv.shape[1]
    tq = Q_BLOCK
    nq = 2 * A_Q_HEADS * HEAD_DIM
    w = 2 * LANES
    band = min(3 * Q_BLOCK, s)
    group = lambda rows, k: pl.BlockSpec((1, rows, w), lambda i, j: (i, 0, nq // w + k))
    return pl.pallas_call(
        functools.partial(_even_attn_kernel, seq=s, band=band),
        out_shape=jax.ShapeDtypeStruct((b, s, nq), BF16),
        grid=(b, s // tq),
        in_specs=[pl.BlockSpec(memory_space=pltpu.SMEM),
                  pl.BlockSpec((1, tq, nq), lambda i, j: (i, j, 0)),
                  group(s, 0), group(s, 1), group(s, 2), group(c, 0), group(c, 1), group(c, 2)],
        out_specs=pl.BlockSpec((1, tq, nq), lambda i, j: (i, j, 0)),
        compiler_params=_params("parallel", "parallel"),
        name="even_attention",
    )(sink, qkv, qkv, qkv, qkv, ckv, ckv, ckv)


def _even_ctx_attn_kernel(sink_ref, q_ref, ck_ref, cva_ref, cvb_ref, o_ref):
    tq = q_ref.shape[1]
    n_pairs = A_Q_HEADS * HEAD_DIM // LANES
    cka, ckb = ck_ref[0, :, 0:LANES], ck_ref[0, :, LANES:2 * LANES]
    cva, cvb = cva_ref[0], cvb_ref[0]
    off = n_pairs * LANES
    for p in range(n_pairs):
        qs = _stack_pair(q_ref[0, :, p * LANES:(p + 1) * LANES])
        o = _softmax_pv([_nt_dot(qs, cka)], [cva])
        o_ref[0, :, p * LANES:(p + 1) * LANES] = _unstack_pair(o).astype(o_ref.dtype)
        h0, h1 = _pair_heads(p)
        qs = _stack_pair(q_ref[0, :, off + p * LANES:off + (p + 1) * LANES])
        o = _softmax_pv([_nt_dot(qs, ckb)], [cvb], _pair_sink(sink_ref, h0, h1, tq))
        o_ref[0, :, off + p * LANES:off + (p + 1) * LANES] = _unstack_pair(o).astype(o_ref.dtype)


def _even_ctx_attention(ckv, sink):
    b, c, _ = ckv.shape
    nq = 2 * A_Q_HEADS * HEAD_DIM
    w = 2 * LANES
    group = lambda k: pl.BlockSpec((1, c, w), lambda i: (i, 0, nq // w + k))
    return pl.pallas_call(
        _even_ctx_attn_kernel,
        out_shape=jax.ShapeDtypeStruct((b, c, nq), BF16),
        grid=(b,),
        in_specs=[pl.BlockSpec(memory_space=pltpu.SMEM),
                  pl.BlockSpec((1, c, nq), lambda i: (i, 0, 0)),
                  group(0), group(1), group(2)],
        out_specs=pl.BlockSpec((1, c, nq), lambda i: (i, 0, 0)),
        compiler_params=_params("parallel"),
        name="even_ctx_attention",
    )(sink, ckv, ckv, ckv, ckv)


def _na_kernel(q_ref, k_ref, v_ref, ck_ref, cv_ref, *rest, rows, kh):
    bias_refs, o_ref = rest[:-1], rest[-1]
    n_loc = kh * GRID_W
    for i, bias_ref in enumerate(bias_refs):
        r = pl.program_id(1) * len(bias_refs) + i
        rs = jnp.clip(r - kh // 2, 0, rows - kh)
        start = pl.multiple_of(rs * GRID_W, GRID_W)
        qrows = slice(i * GRID_W, (i + 1) * GRID_W)
        for p in range(C_HEADS * HEAD_DIM // LANES):
            sl = slice(p * LANES, (p + 1) * LANES)
            vsl = slice(2 * p * LANES, 2 * (p + 1) * LANES)
            qs = _stack_pair(q_ref[0, qrows, sl])
            kb = k_ref[0, pl.ds(start, n_loc), sl]
            vb = v_ref[0, pl.ds(start, n_loc), vsl]
            s_loc = _nt_dot(qs, kb) + bias_ref[0, p]
            o = _softmax_pv([s_loc, _nt_dot(qs, ck_ref[0, :, sl])], [vb, cv_ref[0, :, vsl]])
            o_ref[0, qrows, sl] = _unstack_pair(o).astype(o_ref.dtype)


def _na_bias_table(rpb, rows):
    kh = min(NA_ROWS, rows)
    cols = np.arange(GRID_W)
    col_start = np.clip(cols - NA_COLS // 2, 0, GRID_W - NA_COLS)
    col_in = (cols[None, :] >= col_start[:, None]) & (cols[None, :] < col_start[:, None] + NA_COLS)
    pad = GRID_W - NA_COLS
    padded = jnp.pad(rpb.astype(F32), ((0, 0), (0, 0), (pad, pad)), mode="edge")
    span = padded.shape[-1]
    tiled = jnp.tile(jnp.pad(padded, ((0, 0), (0, 0), (0, 1))), (1, 1, GRID_W))[..., :GRID_W * span]
    t = tiled.reshape(padded.shape[:2] + (GRID_W, span))[..., GRID_W - 1:]
    t = jnp.where(col_in[None, None], t * LOG2E, NEG_INF)
    bands = []
    for d in range(kh):
        band = t[:, NA_ROWS - 1 - d:NA_ROWS - 1 - d + kh]
        bands.append(jnp.swapaxes(band, 1, 2).reshape(C_HEADS // 2, 2 * GRID_W, kh * GRID_W))
    return jnp.stack(bands, axis=0)


def _na_attention(qkv, ckv, bias, rows):
    b, s, _ = qkv.shape
    c = ckv.shape[1]
    hd = C_HEADS * HEAD_DIM
    kh = min(NA_ROWS, rows)

    rps = 2 if rows % 2 == 0 else 1

    def bias_spec(k):
        def bias_map(i, g):
            r = g * rps + k
            return (r - jnp.clip(r - kh // 2, 0, rows - kh), 0, 0, 0)
        return pl.BlockSpec((1,) + bias.shape[1:], bias_map)

    return pl.pallas_call(
        functools.partial(_na_kernel, rows=rows, kh=kh),
        out_shape=jax.ShapeDtypeStruct((b, s, hd), BF16),
        grid=(b, rows // rps),
        in_specs=[pl.BlockSpec((1, rps * GRID_W, hd), lambda i, g: (i, g, 0)),
                  pl.BlockSpec((1, s, hd), lambda i, g: (i, 0, 1)),
                  pl.BlockSpec((1, s, 2 * hd), lambda i, g: (i, 0, 1)),
                  pl.BlockSpec((1, c, hd), lambda i, g: (i, 0, 2)),
                  pl.BlockSpec((1, c, 2 * hd), lambda i, g: (i, 0, 0))] + [bias_spec(k) for k in range(rps)],
        out_specs=pl.BlockSpec((1, rps * GRID_W, hd), lambda i, g: (i, g, 0)),
        compiler_params=_params("parallel", "parallel"),
        name="na_attention",
    )(qkv, qkv, qkv, ckv, ckv, *([bias] * rps))


def _outproj_core(o_ref, w_ref, x_ref, m_ref, gpost_ref, gpre_ref):
    y = _dot(o_ref[0], w_ref[...])
    x = x_ref[0] + m_ref[0, 2:3, :] * (_rms(y) * gpost_ref[...])
    h = (_rms(x) * gpre_ref[...]) * (1.0 + m_ref[0, 4:5, :]) + m_ref[0, 3:4, :]
    return x, h


def _outproj_kernel(o_ref, w_ref, x_ref, m_ref, gpost_ref, gpre_ref, xo_ref, h_ref):
    x, h = _outproj_core(o_ref, w_ref, x_ref, m_ref, gpost_ref, gpre_ref)
    xo_ref[0] = x
    h_ref[0] = h.astype(h_ref.dtype)


def _split_bf16(a):
    hi = a.astype(BF16)
    return hi, (a - hi.astype(F32)).astype(BF16)


def _outproj_router_kernel(o_ref, w_ref, x_ref, m_ref, gpost_ref, gpre_ref, wr_ref,
                           xo_ref, h_ref, rank_ref, gate_ref, rank_t_ref, cnt_ref):
    x, h = _outproj_core(o_ref, w_ref, x_ref, m_ref, gpost_ref, gpre_ref)
    xo_ref[0] = x
    h_ref[0] = h.astype(h_ref.dtype)
    h_hi, h_lo = _split_bf16(h)
    both = _dot(h_hi, wr_ref[...])
    logits = both[:, :LANES] + (both[:, LANES:] + _dot(h_lo, wr_ref[:, :LANES]))
    t = logits.shape[0]
    lane = lax.broadcasted_iota(jnp.int32, logits.shape, 1).astype(F32)
    logits = jnp.where(lane < N_EXPERTS, logits, -jnp.inf)
    v1 = jnp.max(logits, axis=-1, keepdims=True)
    i1 = jnp.min(jnp.where(logits == v1, lane, float(LANES)), axis=-1, keepdims=True)
    rest = jnp.where(lane == i1, -jnp.inf, logits)
    v2 = jnp.max(rest, axis=-1, keepdims=True)
    i2 = jnp.min(jnp.where(rest == v2, lane, float(LANES)), axis=-1, keepdims=True)
    e2 = jnp.exp(v2 - v1)
    w1 = 1.0 / (1.0 + e2)
    w2 = e2 / (1.0 + e2)
    sel1, sel2 = lane == i1, lane == i2
    sel = (sel1 | sel2).astype(F32)
    rr = lax.broadcasted_iota(jnp.int32, (t, t), 0)
    cc = lax.broadcasted_iota(jnp.int32, (t, t), 1)
    before = _dot(jnp.where(rr > cc, 1.0, 0.0).astype(BF16), sel.astype(BF16))
    rank = jnp.where(sel > 0.0, before, -1.0)
    rank_ref[0] = rank[:, :N_EXPERTS]
    gate_ref[0] = jnp.where(sel1, w1, jnp.where(sel2, w2, 0.0))[:, :N_EXPERTS]
    rank_t_ref[0] = rank.T[:N_EXPERTS, :]
    cnt_ref[0] = jnp.sum(sel, axis=0, keepdims=True)


def _outproj(o, w, x, mods, mod_off, g_post, g_pre, *, router=None, name):
    grp, n, d = x.shape
    dm = o.shape[2]
    tm = _tile(n, 512)
    tok = lambda b, t: (b, t, 0)
    const = lambda b, t: (0, 0)
    in_specs = [pl.BlockSpec((1, tm, dm), tok),
                pl.BlockSpec((dm, d), const),
                pl.BlockSpec((1, tm, d), tok),
                pl.BlockSpec((1, N_MOD, d), lambda b, t: (b + mod_off, 0, 0)),
                pl.BlockSpec((1, d), const),
                pl.BlockSpec((1, d), const)]
    args = [o, w, x, mods, g_post.reshape(1, d), g_pre.reshape(1, d)]
    if router is None:
        return pl.pallas_call(
            _outproj_kernel,
            out_shape=(jax.ShapeDtypeStruct((grp, n, d), F32), jax.ShapeDtypeStruct((grp, n, d), BF16)),
            grid=(grp, n // tm),
            in_specs=in_specs,
            out_specs=(pl.BlockSpec((1, tm, d), tok), pl.BlockSpec((1, tm, d), tok)),
            compiler_params=_params("parallel", "parallel"),
            name=name,
        )(*args)
    in_specs += [pl.BlockSpec((d, 2 * LANES), const)]
    args += [jnp.concatenate(router, axis=1)]
    per = n // tm
    blk = lambda b, t: (b * per + t, 0, 0)
    ne = N_EXPERTS
    return pl.pallas_call(
        _outproj_router_kernel,
        out_shape=(jax.ShapeDtypeStruct((grp, n, d), F32), jax.ShapeDtypeStruct((grp, n, d), BF16),
                   jax.ShapeDtypeStruct((grp, n, ne), F32), jax.ShapeDtypeStruct((grp, n, ne), F32),
                   jax.ShapeDtypeStruct((grp * per, ne, tm), F32), jax.ShapeDtypeStruct((grp * per, 1, LANES), F32)),
        grid=(grp, per),
        in_specs=in_specs,
        out_specs=(pl.BlockSpec((1, tm, d), tok), pl.BlockSpec((1, tm, d), tok),
                   pl.BlockSpec((1, tm, ne), tok), pl.BlockSpec((1, tm, ne), tok),
                   pl.BlockSpec((1, ne, tm), blk), pl.BlockSpec((1, 1, LANES), blk)),
        compiler_params=_params("parallel", "parallel"),
        name=name,
    )(*args)


def _swiglu_chunk(h, wg, wu, wd):
    g = _dot(h, wg)
    u = _dot(h, wu)
    a = (g / (1.0 + jnp.exp(-g))) * u
    return _dot(a.astype(BF16), wd)


SUB_ROWS = 256


def _accumulate_swiglu(c, rows_of, acc_ref, wg, wu, wd, n_rows, out_ref=None):
    sub = min(SUB_ROWS, n_rows)
    for r in range(n_rows // sub):
        rows = slice(r * sub, (r + 1) * sub)
        total = jnp.where(c > 0, acc_ref[rows], 0.0) + _swiglu_chunk(rows_of(rows), wg, wu, wd)
        acc_ref[rows] = total
        if out_ref is not None:
            out_ref[rows] = total.astype(out_ref.dtype)


def _ffn_kernel(h_ref, wg_ref, wu_ref, wd_ref, x_ref, m_ref, g_ref, o_ref):
    n_rows = h_ref.shape[1]
    sub = min(SUB_ROWS, n_rows)
    for r in range(n_rows // sub):
        rows = slice(r * sub, (r + 1) * sub)
        y = _swiglu_chunk(h_ref[0, rows], wg_ref[...], wu_ref[...], wd_ref[...])
        o_ref[0, rows] = x_ref[0, rows] + m_ref[0, 5:6, :] * (_rms(y) * g_ref[...])


def _ffn(h, wg, wu, wd, x, mods, mod_off, g_post, *, name):
    grp, n, d = x.shape
    ff = wg.shape[1]
    tm = _tile(n, 1024)
    tok = lambda b, t: (b, t, 0)
    const = lambda b, t: (0, 0)
    resident = dict(index_map=const, pipeline_mode=pl.Buffered(1))
    return pl.pallas_call(
        _ffn_kernel,
        out_shape=jax.ShapeDtypeStruct((grp, n, d), F32),
        grid=(grp, n // tm),
        in_specs=[pl.BlockSpec((1, tm, d), tok),
                  pl.BlockSpec((d, ff), **resident),
                  pl.BlockSpec((d, ff), **resident),
                  pl.BlockSpec((ff, d), **resident),
                  pl.BlockSpec((1, tm, d), tok),
                  pl.BlockSpec((1, N_MOD, d), lambda b, t: (b + mod_off, 0, 0)),
                  pl.BlockSpec((1, d), const)],
        out_specs=pl.BlockSpec((1, tm, d), tok),
        compiler_params=_params("parallel", "parallel"),
        name=name,
    )(h, wg, wu, wd, x, mods, g_post.reshape(1, d))


ROW_ALIGN = 16


def _for_pieces(n, max_piece, fn):
    off = 0
    size = max_piece
    while size >= ROW_ALIGN:
        bit = (n & size) != 0

        @pl.when(bit)
        def _(off=off, size=size):
            fn(pl.multiple_of(off, ROW_ALIGN), size)

        off = off + jnp.where(bit, size, 0)
        size //= 2


def _dispatch_kernel(start_ref, n_ref, zstart_ref, zn_ref, h_ref, rank_t_ref, xs_ref, stage_ref, zero_ref,
                     sem, zsem):
    b = pl.program_id(0)
    last = pl.num_programs(0) - 1
    tm = h_ref.shape[0]
    half = tm // 2
    h = h_ref[...]

    def out_copy(blk, e, off, size):
        start = start_ref[blk * N_EXPERTS + e]
        dst = xs_ref.at[pl.ds(pl.multiple_of(start + off, ROW_ALIGN), size)]
        return pltpu.make_async_copy(stage_ref.at[blk % 2, e, pl.ds(off, size)], dst, sem.at[blk % 2, e])

    def for_chunk_copies(blk, op):
        for e in range(N_EXPERTS):
            _for_pieces(n_ref[blk * N_EXPERTS + e], tm, lambda off, size, e=e: op(out_copy(blk, e, off, size)))

    def zero_copy(start, off, size):
        dst = xs_ref.at[pl.ds(pl.multiple_of(start + off, ROW_ALIGN), size)]
        return pltpu.make_async_copy(zero_ref.at[pl.ds(0, size)], dst, zsem)

    @pl.when(b == 0)
    def _():
        zero_ref[...] = jnp.zeros_like(zero_ref)
        dead_from = zn_ref[N_EXPERTS]
        n_slabs = xs_ref.shape[0] // tm

        def slab(i):
            return zero_copy(pl.multiple_of(i * tm, tm), 0, tm)

        for k in range(N_EXPERTS):
            _for_pieces(zn_ref[k], tm, lambda off, size, k=k: zero_copy(zstart_ref[k], off, size).start())
        lax.fori_loop(dead_from, n_slabs, lambda i, carry: (slab(i).start(), carry)[1], 0)
        for k in range(N_EXPERTS):
            _for_pieces(zn_ref[k], tm, lambda off, size, k=k: zero_copy(zstart_ref[k], off, size).wait())
        lax.fori_loop(dead_from, n_slabs, lambda i, carry: (slab(i).wait(), carry)[1], 0)

    for e in range(N_EXPERTS):
        n = n_ref[b * N_EXPERTS + e]
        rank_row = rank_t_ref[0, e:e + 1, :]
        for hf in range(2):
            @pl.when(n > hf * half)
            def _(hf=hf, e=e, rank_row=rank_row):
                r = lax.broadcasted_iota(jnp.int32, (half, tm), 0).astype(F32) + float(hf * half)
                onehot = jnp.where(r == rank_row, 1.0, 0.0).astype(BF16)
                stage_ref[b % 2, e, hf * half:(hf + 1) * half, :] = _dot(onehot, h).astype(stage_ref.dtype)
    for_chunk_copies(b, lambda cp: cp.start())

    @pl.when(b > 0)
    def _():
        for_chunk_copies(b - 1, lambda cp: cp.wait())

    @pl.when(b == last)
    def _():
        for_chunk_copies(b, lambda cp: cp.wait())


def _dispatch(h, rank_t, starts, counts, zstarts, zcounts, n_rows):
    n, d = h.shape
    nb, _, tm = rank_t.shape
    return pl.pallas_call(
        _dispatch_kernel,
        out_shape=jax.ShapeDtypeStruct((n_rows, d), BF16),
        grid_spec=pltpu.PrefetchScalarGridSpec(
            num_scalar_prefetch=4,
            grid=(nb,),
            in_specs=[pl.BlockSpec((tm, d), lambda i, *_: (i, 0)),
                      pl.BlockSpec((1, N_EXPERTS, tm), lambda i, *_: (i, 0, 0))],
            out_specs=pl.BlockSpec(memory_space=pl.ANY),
            scratch_shapes=[pltpu.VMEM((2, N_EXPERTS, tm, d), BF16), pltpu.VMEM((tm, d), BF16),
                            pltpu.SemaphoreType.DMA((2, N_EXPERTS)), pltpu.SemaphoreType.DMA]),
        compiler_params=_params("arbitrary"),
        name="moe_dispatch",
    )(starts, counts, zstarts, zcounts, h, rank_t)


def _experts_kernel(te_ref, nt_ref, xs_ref, wg_ref, wu_ref, wd_ref, ys_ref, acc_ref):
    i = pl.program_id(0)
    c = pl.program_id(1)

    @pl.when(i < nt_ref[0])
    def _():
        _accumulate_swiglu(c, lambda rows: xs_ref[rows], acc_ref, wg_ref[0], wu_ref[0], wd_ref[0],
                           acc_ref.shape[0], out_ref=ys_ref)

    @pl.when((i >= nt_ref[0]) & (c == 0))
    def _():
        ys_ref[...] = jnp.zeros_like(ys_ref)


def _experts(xs, tile_expert, n_tiles, wg, wu, wd, tmg):
    n_rows, d = xs.shape
    ff = wg.shape[2]
    fc = _tile(ff, 1792)
    total = n_rows // tmg

    def row_map(i, c, te, nt):
        return (jnp.minimum(i, nt[0] - 1), 0)

    def w_in_map(i, c, te, nt):
        return (te[jnp.minimum(i, nt[0] - 1)], 0, jnp.where(i < nt[0], c, ff // fc - 1))

    def w_out_map(i, c, te, nt):
        return (te[jnp.minimum(i, nt[0] - 1)], jnp.where(i < nt[0], c, ff // fc - 1), 0)

    return pl.pallas_call(
        _experts_kernel,
        out_shape=jax.ShapeDtypeStruct((n_rows, d), BF16),
        grid_spec=pltpu.PrefetchScalarGridSpec(
            num_scalar_prefetch=2,
            grid=(total, ff // fc),
            in_specs=[pl.BlockSpec((tmg, d), row_map),
                      pl.BlockSpec((1, d, fc), w_in_map),
                      pl.BlockSpec((1, d, fc), w_in_map),
                      pl.BlockSpec((1, fc, d), w_out_map)],
            out_specs=pl.BlockSpec((tmg, d), lambda i, c, te, nt: (i, 0)),
            scratch_shapes=[pltpu.VMEM((tmg, d), F32)]),
        compiler_params=_params("arbitrary", "arbitrary"),
        name="moe_experts",
    )(tile_expert, n_tiles, xs, wg, wu, wd)


def _combine_kernel(start_ref, n_ref, ys_ref, rank_ref, gate_ref, x_ref, m_ref, g_ref, o_ref, buf_ref, acc_ref, sem):
    per = pl.num_programs(1)
    b = pl.program_id(0) * per + pl.program_id(1)
    tm = x_ref.shape[1]
    half = tm // 2

    n_blocks = pl.num_programs(0) * per

    def in_copy(blk, e, off, size):
        start = start_ref[blk * N_EXPERTS + e]
        src = ys_ref.at[pl.ds(pl.multiple_of(start + off, ROW_ALIGN), size)]
        return pltpu.make_async_copy(src, buf_ref.at[blk % 2, e, pl.ds(off, size)], sem.at[blk % 2, e])

    def fetch(blk):
        for e in range(N_EXPERTS):
            _for_pieces(n_ref[blk * N_EXPERTS + e], tm, lambda off, size, e=e: in_copy(blk, e, off, size).start())

    @pl.when(b == 0)
    def _():
        buf_ref[...] = jnp.zeros_like(buf_ref)
        fetch(b)

    @pl.when(b + 1 < n_blocks)
    def _():
        fetch(b + 1)

    acc_ref[...] = jnp.zeros_like(acc_ref)
    rank = rank_ref[0]
    gate = gate_ref[0]
    for e in range(N_EXPERTS):
        n = n_ref[b * N_EXPERTS + e]
        _for_pieces(n, tm, lambda off, size, e=e: in_copy(b, e, off, size).wait())
        for hf in range(2):
            @pl.when(n > hf * half)
            def _(hf=hf, e=e):
                col = lax.broadcasted_iota(jnp.int32, (tm, half), 1).astype(F32) + float(hf * half)
                onehot = jnp.where(rank[:, e:e + 1] == col, 1.0, 0.0).astype(BF16)
                y = buf_ref[b % 2, e, hf * half:(hf + 1) * half, :]
                acc_ref[...] += gate[:, e:e + 1] * _dot(onehot, y)
    o_ref[0] = x_ref[0] + m_ref[0, 5:6, :] * (_rms(acc_ref[...]) * g_ref[...])


def _combine(ys, starts, counts, rank, gate, x, mods, g_post, tm):
    b, s, d = x.shape
    per = s // tm
    tok = lambda i, t, *_: (i, t, 0)
    return pl.pallas_call(
        _combine_kernel,
        out_shape=jax.ShapeDtypeStruct((b, s, d), F32),
        grid_spec=pltpu.PrefetchScalarGridSpec(
            num_scalar_prefetch=2,
            grid=(b, per),
            in_specs=[pl.BlockSpec(memory_space=pl.ANY),
                      pl.BlockSpec((1, tm, N_EXPERTS), tok),
                      pl.BlockSpec((1, tm, N_EXPERTS), tok),
                      pl.BlockSpec((1, tm, d), tok),
                      pl.BlockSpec((1, N_MOD, d), lambda i, t, *_: (i, 0, 0)),
                      pl.BlockSpec((1, d), lambda i, t, *_: (0, 0))],
            out_specs=pl.BlockSpec((1, tm, d), tok),
            scratch_shapes=[pltpu.VMEM((2, N_EXPERTS, tm, d), BF16), pltpu.VMEM((tm, d), F32),
                            pltpu.SemaphoreType.DMA((2, N_EXPERTS))]),
        compiler_params=_params("arbitrary", "arbitrary"),
        name="moe_combine",
    )(starts, counts, ys, rank, gate, x, mods, g_post.reshape(1, d))


def _moe(h, rank, gate, rank_t, cnt, x, mods, g_post, wg, wu, wd):
    b, s, d = x.shape
    n = b * s
    nb, _, tm = rank_t.shape
    tmg = _tile(2 * n, 1024)
    total_tiles = -(-(2 * n + (ROW_ALIGN - 1) * N_EXPERTS * nb) // tmg) + N_EXPERTS
    n_rows = total_tiles * tmg
    per_block = cnt[:, 0, :N_EXPERTS].astype(jnp.int32)
    chunk = (per_block + ROW_ALIGN - 1) // ROW_ALIGN * ROW_ALIGN
    rows_e = jnp.sum(chunk, axis=0)
    tiles_per = (rows_e + tmg - 1) // tmg
    tile_end = jnp.cumsum(tiles_per)
    offsets = (tile_end - tiles_per) * tmg
    starts = (offsets[None, :] + jnp.cumsum(chunk, axis=0) - chunk).reshape(nb * N_EXPERTS)
    counts = chunk.reshape(nb * N_EXPERTS)
    zstarts = offsets + rows_e
    zcounts = jnp.concatenate([tiles_per * tmg - rows_e, tile_end[-1:] * (tmg // tm)])
    tile_expert = jnp.sum((tile_end[None, :] <= jnp.arange(total_tiles)[:, None]).astype(jnp.int32), axis=1)
    tile_expert = jnp.minimum(tile_expert, N_EXPERTS - 1)
    n_tiles = tile_end[-1:].astype(jnp.int32)
    xs = _dispatch(h.reshape(n, d), rank_t, starts, counts, zstarts, zcounts.astype(jnp.int32), n_rows)
    ys = _experts(xs, tile_expert, n_tiles, wg, wu, wd, tmg)
    return _combine(ys, starts, counts, rank, gate, x, mods, g_post, tm)


def _even_layouts(w_in, w_out, g_q, g_k, sink):
    dh, half = HEAD_DIM, HEAD_DIM // 2
    within = np.concatenate([np.arange(0, dh, 2), np.arange(1, dh, 2)])
    g = A_Q_HEADS // A_KV_HEADS
    q_order = np.array([h for p in range(g) for h in _pair_heads(p)])
    sizes = [A_Q_HEADS * dh, A_KV_HEADS * dh, A_KV_HEADS * dh, B_Q_HEADS * dh, B_KV_HEADS * dh, B_KV_HEADS * dh]
    base = np.concatenate([[0], np.cumsum(sizes)])

    def head_cols(start, order, permute):
        inner = within if permute else np.arange(dh)
        return np.concatenate([start + h * dh + inner for h in order])

    kv_order = np.arange(A_KV_HEADS)
    cols = np.concatenate([head_cols(base[0], q_order, True), head_cols(base[3], q_order, True),
                           head_cols(base[1], kv_order, True), head_cols(base[4], kv_order, True),
                           head_cols(base[2], kv_order, False), head_cols(base[5], kv_order, False)])
    rows_out = np.concatenate([head_cols(0, q_order, False), head_cols(A_Q_HEADS * dh, q_order, False)])
    gq = jnp.tile(g_q[within], LANES // dh).reshape(1, LANES)
    gk = jnp.tile(g_k[within], LANES // dh).reshape(1, LANES)
    lane = np.arange(LANES)
    head_mean = jnp.asarray((lane[:, None] // dh == lane[None, :] // dh) / dh, BF16)
    return (w_in[:, cols].astype(BF16), w_out[rows_out].astype(BF16), gq, gk, head_mean,
            sink.astype(F32) * LOG2E)


def _rope_tables(n_tokens):
    t = jnp.arange(n_tokens, dtype=jnp.int32)
    row = (t // GRID_W).astype(F32)
    col = (t % GRID_W).astype(F32)
    half = HEAD_DIM // 2
    freqs = ROPE_THETA ** (-jnp.arange(0, half, 2, dtype=F32) / half)
    ang = jnp.concatenate([row[:, None] * freqs, col[:, None] * freqs], axis=-1)
    cos, sin = jnp.cos(ang), jnp.sin(ang)
    reps = LANES // HEAD_DIM
    return (jnp.tile(jnp.concatenate([cos, cos], axis=-1), (1, reps)),
            jnp.tile(jnp.concatenate([-sin, sin], axis=-1), (1, reps)))


def kernel(x, c, ctx, c_ctx, e_w_mod, e_b_mod, e_g_pre_mix, e_g_post_mix, e_g_pre_ffn, e_g_post_ffn, e_w_in, e_w_out, e_g_q, e_g_k, e_sink, e_w_gate, e_w_up, e_w_down, o_w_mod, o_b_mod, o_g_pre_mix, o_g_post_mix, o_g_pre_ffn, o_g_post_ffn, o_w_in, o_w_out, o_rpb, o_w_router, o_w_gate, o_w_up, o_w_down):
    b, s, d = x.shape
    n_ctx = ctx.shape[1]
    rows = s // GRID_W
    ctx = ctx.reshape(1, b * n_ctx, d)

    mods = _mod_vectors(c, c_ctx, e_w_mod[0], e_b_mod[0])
    w_in, w_out, gq, gk, head_mean, sink = _even_layouts(e_w_in[0], e_w_out[0], e_g_q[0], e_g_k[0], e_sink[0])
    even = (gq, gk, head_mean)
    qkv = _inproj(x, mods, 0, e_g_pre_mix[0], w_in, even=even, rope_tabs=_rope_tables(s), name="inproj0_lat")
    ckv = _inproj(ctx, mods, b, e_g_pre_mix[0], w_in, even=even, rope_tabs=(None, None), name="inproj0_ctx")
    ckv = ckv.reshape(b, n_ctx, -1)
    o_lat = _even_attention(qkv, ckv, sink)
    o_ctx = _even_ctx_attention(ckv, sink).reshape(1, b * n_ctx, -1)
    wg, wu, wd = e_w_gate[0].astype(BF16), e_w_up[0].astype(BF16), e_w_down[0].astype(BF16)
    x, h = _outproj(o_lat, w_out, x, mods, 0, e_g_post_mix[0], e_g_pre_ffn[0], name="outproj0_lat")
    x = _ffn(h, wg, wu, wd, x, mods, 0, e_g_post_ffn[0], name="ffn0_lat")
    ctx, h = _outproj(o_ctx, w_out, ctx, mods, b, e_g_post_mix[0], e_g_pre_ffn[0], name="outproj0_ctx")
    ctx = _ffn(h, wg, wu, wd, ctx, mods, b, e_g_post_ffn[0], name="ffn0_ctx")

    mods = _mod_vectors(c, c_ctx, o_w_mod[0], o_b_mod[0])
    hd = C_HEADS * HEAD_DIM
    w_in = o_w_in[0].astype(BF16)
    qkv = _inproj(x, mods, 0, o_g_pre_mix[0], w_in, segments=(("q", hd), ("k", hd), ("v", hd)), name="inproj1_lat")
    w_vk = jnp.concatenate([w_in[:, 2 * hd:], w_in[:, hd:2 * hd]], axis=1)
    ckv = _inproj(ctx, mods, b, o_g_pre_mix[0], w_vk, segments=(("v", hd), ("k", hd)), name="inproj1_ctx")
    ckv = ckv.reshape(b, n_ctx, 3 * hd)
    o_lat = _na_attention(qkv, ckv, _na_bias_table(o_rpb[0], rows), rows)
    wr = jnp.zeros((d, LANES), F32).at[:, :N_EXPERTS].set(o_w_router[0])
    x, h, rank, gate, rank_t, cnt = _outproj(o_lat, o_w_out[0].astype(BF16), x, mods, 0, o_g_post_mix[0],
                                             o_g_pre_ffn[0], router=_split_bf16(wr), name="outproj1_router")
    return _moe(h, rank, gate, rank_t, cnt, x, mods, o_g_post_ffn[0],
                o_w_gate[0].astype(BF16), o_w_up[0].astype(BF16), o_w_down[0].astype(BF16))
```

```python
import functools

import numpy as np
import jax
import jax.numpy as jnp
from jax import lax
from jax.experimental import pallas as pl
from jax.experimental.pallas import tpu as pltpu

GRID_W = 64
HEAD_DIM = 64
A_Q_HEADS = 8
A_KV_HEADS = 2
B_Q_HEADS = 8
B_KV_HEADS = 2
C_HEADS = 16
Q_BLOCK = 128
WINDOW = 128
NA_ROWS = 8
NA_COLS = 16
ROPE_THETA = 10000.0
N_EXPERTS = 8
N_MOD = 6
EPS = 1e-6
NEG_INF = -1e30
ATTN_SCALE = HEAD_DIM ** -0.5
LOG2E = 1.4426950408889634
Q_SCALE = ATTN_SCALE * LOG2E

LANES = 128
VMEM_LIMIT = 56 * 1024 * 1024

F32 = jnp.float32
BF16 = jnp.bfloat16
HIGHEST = lax.Precision.HIGHEST


def _tile(n, pref):
    return pref if n % pref == 0 else n


def _params(*sem):
    return pltpu.CompilerParams(dimension_semantics=sem, vmem_limit_bytes=VMEM_LIMIT)


def _rms(x):
    return x * lax.rsqrt(jnp.mean(x * x, axis=-1, keepdims=True) + EPS)


def _nt_dot(a, b):
    return lax.dot_general(a, b, (((1,), (1,)), ((), ())), preferred_element_type=F32)


def _dot(a, b):
    return jnp.dot(a, b, preferred_element_type=F32)


def _mod_kernel(c_ref, w_ref, b_ref, o_ref):
    c = c_ref[...]
    a = c / (1.0 + jnp.exp(-c))
    o_ref[...] = jnp.dot(a, w_ref[...], precision=HIGHEST, preferred_element_type=F32) + b_ref[...]


def _mod_vectors(c, c_ctx, w_mod, b_mod):
    b, d = c.shape
    rows = -(-(b + 1) // 8) * 8
    cc = jnp.zeros((rows, d), F32).at[:b].set(c).at[b].set(c_ctx)
    n = w_mod.shape[1]
    tn = _tile(n, 1536)
    m = pl.pallas_call(
        _mod_kernel,
        out_shape=jax.ShapeDtypeStruct((rows, n), F32),
        grid=(n // tn,),
        in_specs=[pl.BlockSpec((rows, d), lambda j: (0, 0)),
                  pl.BlockSpec((d, tn), lambda j: (0, j)),
                  pl.BlockSpec((1, tn), lambda j: (0, j))],
        out_specs=pl.BlockSpec((rows, tn), lambda j: (0, j)),
        compiler_params=_params("arbitrary"),
        name="mod_vectors",
    )(cc, w_mod, b_mod.reshape(1, n))
    return m.reshape(rows, N_MOD, d)


def _modulated(x_ref, m_ref, g_ref):
    x = x_ref[0]
    return (_rms(x) * g_ref[...]) * (1.0 + m_ref[0, 1:2, :]) + m_ref[0, 0:1, :]


def _rope_block(y, cos, sin_signed):
    lane = lax.broadcasted_iota(jnp.int32, y.shape, 1)
    partner = jnp.where(lane % HEAD_DIM < HEAD_DIM // 2,
                        pltpu.roll(y, LANES - HEAD_DIM // 2, 1),
                        pltpu.roll(y, HEAD_DIM // 2, 1))
    return y * cos + partner * sin_signed


def _store_value_block(o_ref, col, vb):
    o_ref[0, :, col:col + LANES] = vb.astype(o_ref.dtype)
    o_ref[0, :, col + LANES:col + 2 * LANES] = jnp.ones(vb.shape, o_ref.dtype)


def _inproj_even_kernel(x_ref, m_ref, g_ref, w_ref, gq_ref, gk_ref, hm_ref, cos_ref, sin_ref, o_ref, *, rope):
    h = _modulated(x_ref, m_ref, g_ref).astype(BF16)
    y = _dot(h, w_ref[...])
    n_qa = A_Q_HEADS * HEAD_DIM // LANES
    n_q = n_qa + B_Q_HEADS * HEAD_DIM // LANES
    for i in range(y.shape[1] // LANES):
        yb = y[:, i * LANES:(i + 1) * LANES]
        is_qa, is_q, is_ka, is_k = i < n_qa, i < n_q, i == n_q, n_q <= i < n_q + 2
        if is_qa or is_ka:
            ms = _dot((yb * yb).astype(BF16), hm_ref[...])
            yb = yb * lax.rsqrt(ms + EPS) * (gq_ref[...] if is_qa else gk_ref[...])
        if rope and (is_q or is_k):
            yb = _rope_block(yb, cos_ref[...], sin_ref[...])
        if is_q:
            yb = yb * Q_SCALE
        if is_q or is_k:
            o_ref[0, :, i * LANES:(i + 1) * LANES] = yb.astype(o_ref.dtype)
        else:
            _store_value_block(o_ref, (n_q + 2 + 2 * (i - n_q - 2)) * LANES, yb)


def _inproj_plain_kernel(x_ref, m_ref, g_ref, w_ref, o_ref, *, segments):
    h = _modulated(x_ref, m_ref, g_ref).astype(BF16)
    y = _dot(h, w_ref[...])
    src = dst = 0
    for kind, width in segments:
        seg = y[:, src:src + width]
        if kind == "v":
            for p in range(width // LANES):
                _store_value_block(o_ref, dst + 2 * p * LANES, seg[:, p * LANES:(p + 1) * LANES])
            dst += 2 * width
        else:
            o_ref[0, :, dst:dst + width] = (seg * Q_SCALE if kind == "q" else seg).astype(o_ref.dtype)
            dst += width
        src += width


def _inproj(x, mods, mod_off, g, w, *, even=None, rope_tabs=None, segments=None, name):
    grp, n, d = x.shape
    if even is not None:
        nout = w.shape[1] + (A_KV_HEADS + B_KV_HEADS) * HEAD_DIM
    else:
        nout = sum(width * (2 if kind == "v" else 1) for kind, width in segments)
    tm = _tile(n, 512)
    grid = (grp, n // tm)
    in_specs = [pl.BlockSpec((1, tm, d), lambda b, t: (b, t, 0)),
                pl.BlockSpec((1, N_MOD, d), lambda b, t: (b + mod_off, 0, 0)),
                pl.BlockSpec((1, d), lambda b, t: (0, 0)),
                pl.BlockSpec(w.shape, lambda b, t: (0, 0))]
    args = [x, mods, g.reshape(1, d), w]
    if even is not None:
        gq, gk, hm = even
        cos, sin = rope_tabs
        rope = cos is not None
        if not rope:
            cos = jnp.zeros((tm, LANES), F32)
            sin = cos
            tab_map = lambda b, t: (0, 0)
        else:
            tab_map = lambda b, t: (t, 0)
        in_specs += [pl.BlockSpec((1, LANES), lambda b, t: (0, 0)),
                     pl.BlockSpec((1, LANES), lambda b, t: (0, 0)),
                     pl.BlockSpec((LANES, LANES), lambda b, t: (0, 0)),
                     pl.BlockSpec((tm, LANES), tab_map),
                     pl.BlockSpec((tm, LANES), tab_map)]
        args += [gq, gk, hm, cos, sin]
        body = functools.partial(_inproj_even_kernel, rope=rope)
    else:
        body = functools.partial(_inproj_plain_kernel, segments=segments)
    return pl.pallas_call(
        body,
        out_shape=jax.ShapeDtypeStruct((grp, n, nout), BF16),
        grid=grid,
        in_specs=in_specs,
        out_specs=pl.BlockSpec((1, tm, nout), lambda b, t: (b, t, 0)),
        compiler_params=_params("parallel", "parallel"),
        name=name,
    )(*args)


def _stack_pair(qp):
    lane = lax.broadcasted_iota(jnp.int32, qp.shape, 1)
    zero = jnp.zeros_like(qp)
    return jnp.concatenate([jnp.where(lane < HEAD_DIM, qp, zero),
                            jnp.where(lane >= HEAD_DIM, qp, zero)], axis=0)


def _unstack_pair(o):
    t = o.shape[0] // 2
    lane = lax.broadcasted_iota(jnp.int32, (t, LANES), 1)
    return jnp.where(lane < HEAD_DIM, o[:t], o[t:])


def _softmax_pv(scores, values, sink=None):
    m = functools.reduce(jnp.maximum, [jnp.max(s, axis=-1, keepdims=True) for s in scores])
    if sink is not None:
        m = jnp.maximum(m, sink)
    acc = None
    for s, v in zip(scores, values):
        pv = _dot(jnp.exp2(s - m).astype(v.dtype), v)
        acc = pv if acc is None else acc + pv
    den = acc[:, LANES:]
    if sink is not None:
        den = den + jnp.exp2(sink - m)
    return acc[:, :LANES] * (1.0 / den)


def _pair_sink(sink_ref, h0, h1, t):
    row = lax.broadcasted_iota(jnp.int32, (2 * t, 1), 0)
    return jnp.where(row < t, sink_ref[h0], sink_ref[h1])


def _pair_heads(p):
    return p, p + A_Q_HEADS // A_KV_HEADS


def _even_attn_kernel(sink_ref, q_ref, k_ref, va_ref, vb_ref, ck_ref, cva_ref, cvb_ref, o_ref, *, seq, band):
    j = pl.program_id(1)
    tq = q_ref.shape[1]
    n_pairs = A_Q_HEADS * HEAD_DIM // LANES
    cka, ckb = ck_ref[0, :, 0:LANES], ck_ref[0, :, LANES:2 * LANES]
    cva, cvb = cva_ref[0], cvb_ref[0]
    ka, va = k_ref[0, :, 0:LANES], va_ref[0]
    for p in range(n_pairs):
        qs = _stack_pair(q_ref[0, :, p * LANES:(p + 1) * LANES])
        o = _softmax_pv([_nt_dot(qs, ka), _nt_dot(qs, cka)], [va, cva])
        o_ref[0, :, p * LANES:(p + 1) * LANES] = _unstack_pair(o).astype(o_ref.dtype)
    start = pl.multiple_of(jnp.clip(j * tq - WINDOW, 0, seq - band), Q_BLOCK)
    kb = k_ref[0, pl.ds(start, band), LANES:2 * LANES]
    vb = vb_ref[0, pl.ds(start, band), :]
    qpos = j * tq + lax.broadcasted_iota(jnp.int32, (tq, band), 0)
    kpos = start + lax.broadcasted_iota(jnp.int32, (tq, band), 1)
    ok = jnp.abs(kpos - qpos) <= WINDOW
    ok = jnp.concatenate([ok, ok], axis=0)
    off = n_pairs * LANES
    for p in range(n_pairs):
        h0, h1 = _pair_heads(p)
        qs = _stack_pair(q_ref[0, :, off + p * LANES:off + (p + 1) * LANES])
        s_win = jnp.where(ok, _nt_dot(qs, kb), NEG_INF)
        o = _softmax_pv([s_win, _nt_dot(qs, ckb)], [vb, cvb], _pair_sink(sink_ref, h0, h1, tq))
        o_ref[0, :, off + p * LANES:off + (p + 1) * LANES] = _unstack_pair(o).astype(o_ref.dtype)


def _even_attention(qkv, ckv, sink):
    b, s, _ = qkv.shape
    c = ckv.shape[1]
    tq = _tile(s, 2 * Q_BLOCK)
    nq = 2 * A_Q_HEADS * HEAD_DIM
    w = 2 * LANES
    band = min(tq + 2 * WINDOW, s)
    group = lambda rows, k: pl.BlockSpec((1, rows, w), lambda i, j: (i, 0, nq // w + k))
    return pl.pallas_call(
        functools.partial(_even_attn_kernel, seq=s, band=band),
        out_shape=jax.ShapeDtypeStruct((b, s, nq), BF16),
        grid=(b, s // tq),
        in_specs=[pl.BlockSpec(memory_space=pltpu.SMEM),
                  pl.BlockSpec((1, tq, nq), lambda i, j: (i, j, 0)),
                  group(s, 0), group(s, 1), group(s, 2), group(c, 0), group(c, 1), group(c, 2)],
        out_specs=pl.BlockSpec((1, tq, nq), lambda i, j: (i, j, 0)),
        compiler_params=_params("parallel", "parallel"),
        name="even_attention",
    )(sink, qkv, qkv, qkv, qkv, ckv, ckv, ckv)


def _even_ctx_attn_kernel(sink_ref, q_ref, ck_ref, cva_ref, cvb_ref, o_ref):
    tq = q_ref.shape[1]
    n_pairs = A_Q_HEADS * HEAD_DIM // LANES
    cka, ckb = ck_ref[0, :, 0:LANES], ck_ref[0, :, LANES:2 * LANES]
    cva, cvb = cva_ref[0], cvb_ref[0]
    off = n_pairs * LANES
    for p in range(n_pairs):
        qs = _stack_pair(q_ref[0, :, p * LANES:(p + 1) * LANES])
        o = _softmax_pv([_nt_dot(qs, cka)], [cva])
        o_ref[0, :, p * LANES:(p + 1) * LANES] = _unstack_pair(o).astype(o_ref.dtype)
        h0, h1 = _pair_heads(p)
        qs = _stack_pair(q_ref[0, :, off + p * LANES:off + (p + 1) * LANES])
        o = _softmax_pv([_nt_dot(qs, ckb)], [cvb], _pair_sink(sink_ref, h0, h1, tq))
        o_ref[0, :, off + p * LANES:off + (p + 1) * LANES] = _unstack_pair(o).astype(o_ref.dtype)


def _even_ctx_attention(ckv, sink):
    b, c, _ = ckv.shape
    nq = 2 * A_Q_HEADS * HEAD_DIM
    w = 2 * LANES
    group = lambda k: pl.BlockSpec((1, c, w), lambda i: (i, 0, nq // w + k))
    return pl.pallas_call(
        _even_ctx_attn_kernel,
        out_shape=jax.ShapeDtypeStruct((b, c, nq), BF16),
        grid=(b,),
        in_specs=[pl.BlockSpec(memory_space=pltpu.SMEM),
                  pl.BlockSpec((1, c, nq), lambda i: (i, 0, 0)),
                  group(0), group(1), group(2)],
        out_specs=pl.BlockSpec((1, c, nq), lambda i: (i, 0, 0)),
        compiler_params=_params("parallel"),
        name="even_ctx_attention",
    )(sink, ckv, ckv, ckv, ckv)


def _na_kernel(q_ref, k_ref, v_ref, ck_ref, cv_ref, *rest, rows, kh):
    bias_refs, o_ref = rest[:-1], rest[-1]
    n_loc = kh * GRID_W
    for i, bias_ref in enumerate(bias_refs):
        r = pl.program_id(1) * len(bias_refs) + i
        rs = jnp.clip(r - kh // 2, 0, rows - kh)
        start = pl.multiple_of(rs * GRID_W, GRID_W)
        qrows = slice(i * GRID_W, (i + 1) * GRID_W)
        for p in range(C_HEADS * HEAD_DIM // LANES):
            sl = slice(p * LANES, (p + 1) * LANES)
            vsl = slice(2 * p * LANES, 2 * (p + 1) * LANES)
            qs = _stack_pair(q_ref[0, qrows, sl])
            kb = k_ref[0, pl.ds(start, n_loc), sl]
            vb = v_ref[0, pl.ds(start, n_loc), vsl]
            s_loc = _nt_dot(qs, kb) + bias_ref[0, p]
            o = _softmax_pv([s_loc, _nt_dot(qs, ck_ref[0, :, sl])], [vb, cv_ref[0, :, vsl]])
            o_ref[0, qrows, sl] = _unstack_pair(o).astype(o_ref.dtype)


def _na_bias_table(rpb, rows):
    kh = min(NA_ROWS, rows)
    cols = np.arange(GRID_W)
    col_start = np.clip(cols - NA_COLS // 2, 0, GRID_W - NA_COLS)
    col_in = (cols[None, :] >= col_start[:, None]) & (cols[None, :] < col_start[:, None] + NA_COLS)
    pad = GRID_W - NA_COLS
    padded = jnp.pad(rpb.astype(F32), ((0, 0), (0, 0), (pad, pad)), mode="edge")
    span = padded.shape[-1]
    tiled = jnp.tile(jnp.pad(padded, ((0, 0), (0, 0), (0, 1))), (1, 1, GRID_W))[..., :GRID_W * span]
    t = tiled.reshape(padded.shape[:2] + (GRID_W, span))[..., GRID_W - 1:]
    t = jnp.where(col_in[None, None], t * LOG2E, NEG_INF)
    bands = []
    for d in range(kh):
        band = t[:, NA_ROWS - 1 - d:NA_ROWS - 1 - d + kh]
        bands.append(jnp.swapaxes(band, 1, 2).reshape(C_HEADS // 2, 2 * GRID_W, kh * GRID_W))
    return jnp.stack(bands, axis=0)


def _na_attention(qkv, ckv, bias, rows):
    b, s, _ = qkv.shape
    c = ckv.shape[1]
    hd = C_HEADS * HEAD_DIM
    kh = min(NA_ROWS, rows)

    rps = next(k for k in (4, 2, 1) if rows % k == 0)

    def bias_spec(k):
        def bias_map(i, g):
            r = g * rps + k
            return (r - jnp.clip(r - kh // 2, 0, rows - kh), 0, 0, 0)
        return pl.BlockSpec((1,) + bias.shape[1:], bias_map)

    return pl.pallas_call(
        functools.partial(_na_kernel, rows=rows, kh=kh),
        out_shape=jax.ShapeDtypeStruct((b, s, hd), BF16),
        grid=(b, rows // rps),
        in_specs=[pl.BlockSpec((1, rps * GRID_W, hd), lambda i, g: (i, g, 0)),
                  pl.BlockSpec((1, s, hd), lambda i, g: (i, 0, 1)),
                  pl.BlockSpec((1, s, 2 * hd), lambda i, g: (i, 0, 1)),
                  pl.BlockSpec((1, c, hd), lambda i, g: (i, 0, 2)),
                  pl.BlockSpec((1, c, 2 * hd), lambda i, g: (i, 0, 0))] + [bias_spec(k) for k in range(rps)],
        out_specs=pl.BlockSpec((1, rps * GRID_W, hd), lambda i, g: (i, g, 0)),
        compiler_params=_params("parallel", "parallel"),
        name="na_attention",
    )(qkv, qkv, qkv, ckv, ckv, *([bias] * rps))


def _outproj_core(o_ref, w_ref, x_ref, m_ref, gpost_ref, gpre_ref):
    y = _dot(o_ref[0], w_ref[...])
    x = x_ref[0] + m_ref[0, 2:3, :] * (_rms(y) * gpost_ref[...])
    h = (_rms(x) * gpre_ref[...]) * (1.0 + m_ref[0, 4:5, :]) + m_ref[0, 3:4, :]
    return x, h


def _outproj_kernel(o_ref, w_ref, x_ref, m_ref, gpost_ref, gpre_ref, xo_ref, h_ref):
    x, h = _outproj_core(o_ref, w_ref, x_ref, m_ref, gpost_ref, gpre_ref)
    xo_ref[0] = x
    h_ref[0] = h.astype(h_ref.dtype)


def _split_bf16(a):
    hi = a.astype(BF16)
    return hi, (a - hi.astype(F32)).astype(BF16)


def _outproj_router_kernel(o_ref, w_ref, x_ref, m_ref, gpost_ref, gpre_ref, wr_ref,
                           xo_ref, h_ref, rank_ref, gate_ref, pos_t_ref, cnt_ref):
    x, h = _outproj_core(o_ref, w_ref, x_ref, m_ref, gpost_ref, gpre_ref)
    xo_ref[0] = x
    h_ref[0] = h.astype(h_ref.dtype)
    h_hi, h_lo = _split_bf16(h)
    both = _dot(h_hi, wr_ref[...])
    logits = both[:, :LANES] + (both[:, LANES:] + _dot(h_lo, wr_ref[:, :LANES]))
    t = logits.shape[0]
    lane = lax.broadcasted_iota(jnp.int32, logits.shape, 1).astype(F32)
    logits = jnp.where(lane < N_EXPERTS, logits, -jnp.inf)
    v1 = jnp.max(logits, axis=-1, keepdims=True)
    i1 = jnp.min(jnp.where(logits == v1, lane, float(LANES)), axis=-1, keepdims=True)
    rest = jnp.where(lane == i1, -jnp.inf, logits)
    v2 = jnp.max(rest, axis=-1, keepdims=True)
    i2 = jnp.min(jnp.where(rest == v2, lane, float(LANES)), axis=-1, keepdims=True)
    e2 = jnp.exp(v2 - v1)
    w1 = 1.0 / (1.0 + e2)
    w2 = e2 / (1.0 + e2)
    sel1, sel2 = lane == i1, lane == i2
    sel = (sel1 | sel2).astype(F32)
    rr = lax.broadcasted_iota(jnp.int32, (t, t), 0)
    cc = lax.broadcasted_iota(jnp.int32, (t, t), 1)
    before = _dot(jnp.where(rr > cc, 1.0, 0.0).astype(BF16), sel.astype(BF16))
    rank = jnp.where(sel > 0.0, before, -1.0)
    rank_ref[0] = rank[:, :N_EXPERTS]
    gate_ref[0] = jnp.where(sel1, w1, jnp.where(sel2, w2, 0.0))[:, :N_EXPERTS]
    cnt = jnp.sum(sel, axis=0, keepdims=True)
    cnt_ref[0] = cnt
    chunk = jnp.floor((cnt + (ROW_ALIGN - 1)) * (1.0 / ROW_ALIGN)) * ROW_ALIGN
    ei = lax.broadcasted_iota(jnp.int32, (LANES, LANES), 0)
    ej = lax.broadcasted_iota(jnp.int32, (LANES, LANES), 1)
    offs = _dot(jnp.broadcast_to(chunk, (8, LANES)).astype(BF16), jnp.where(ei < ej, 1.0, 0.0).astype(BF16))[0:1]
    local = before + offs
    pos1 = jnp.sum(jnp.where(sel1, local, 0.0), axis=-1, keepdims=True)
    pos2 = jnp.sum(jnp.where(sel2, local, 0.0), axis=-1, keepdims=True)
    pos_t_ref[0] = jnp.where(lane == 0.0, pos1, jnp.where(lane == 1.0, pos2, 0.0)).T[:8, :]


def _outproj(o, w, x, mods, mod_off, g_post, g_pre, *, router=None, name):
    grp, n, d = x.shape
    dm = o.shape[2]
    tm = _tile(n, 512)
    tok = lambda b, t: (b, t, 0)
    const = lambda b, t: (0, 0)
    in_specs = [pl.BlockSpec((1, tm, dm), tok),
                pl.BlockSpec((dm, d), const),
                pl.BlockSpec((1, tm, d), tok),
                pl.BlockSpec((1, N_MOD, d), lambda b, t: (b + mod_off, 0, 0)),
                pl.BlockSpec((1, d), const),
                pl.BlockSpec((1, d), const)]
    args = [o, w, x, mods, g_post.reshape(1, d), g_pre.reshape(1, d)]
    if router is None:
        return pl.pallas_call(
            _outproj_kernel,
            out_shape=(jax.ShapeDtypeStruct((grp, n, d), F32), jax.ShapeDtypeStruct((grp, n, d), BF16)),
            grid=(grp, n // tm),
            in_specs=in_specs,
            out_specs=(pl.BlockSpec((1, tm, d), tok), pl.BlockSpec((1, tm, d), tok)),
            compiler_params=_params("parallel", "parallel"),
            name=name,
        )(*args)
    in_specs += [pl.BlockSpec((d, 2 * LANES), const)]
    args += [jnp.concatenate(router, axis=1)]
    per = n // tm
    blk = lambda b, t: (b * per + t, 0, 0)
    ne = N_EXPERTS
    return pl.pallas_call(
        _outproj_router_kernel,
        out_shape=(jax.ShapeDtypeStruct((grp, n, d), F32), jax.ShapeDtypeStruct((grp, n, d), BF16),
                   jax.ShapeDtypeStruct((grp, n, ne), F32), jax.ShapeDtypeStruct((grp, n, ne), F32),
                   jax.ShapeDtypeStruct((grp * per, 8, tm), F32), jax.ShapeDtypeStruct((grp * per, 1, LANES), F32)),
        grid=(grp, per),
        in_specs=in_specs,
        out_specs=(pl.BlockSpec((1, tm, d), tok), pl.BlockSpec((1, tm, d), tok),
                   pl.BlockSpec((1, tm, ne), tok), pl.BlockSpec((1, tm, ne), tok),
                   pl.BlockSpec((1, 8, tm), blk), pl.BlockSpec((1, 1, LANES), blk)),
        compiler_params=_params("parallel", "parallel"),
        name=name,
    )(*args)


def _swiglu_chunk(h, wg, wu, wd):
    g = _dot(h, wg)
    u = _dot(h, wu)
    a = (g / (1.0 + jnp.exp(-g))) * u
    return _dot(a.astype(BF16), wd)


SUB_ROWS = 256


def _accumulate_swiglu(c, rows_of, acc_ref, wg, wu, wd, n_rows, out_ref=None):
    sub = min(SUB_ROWS, n_rows)
    for r in range(n_rows // sub):
        rows = slice(r * sub, (r + 1) * sub)
        total = jnp.where(c > 0, acc_ref[rows], 0.0) + _swiglu_chunk(rows_of(rows), wg, wu, wd)
        acc_ref[rows] = total
        if out_ref is not None:
            out_ref[rows] = total.astype(out_ref.dtype)


def _ffn_kernel(h_ref, wg_ref, wu_ref, wd_ref, x_ref, m_ref, g_ref, o_ref):
    n_rows = h_ref.shape[1]
    sub = min(SUB_ROWS, n_rows)
    for r in range(n_rows // sub):
        rows = slice(r * sub, (r + 1) * sub)
        y = _swiglu_chunk(h_ref[0, rows], wg_ref[...], wu_ref[...], wd_ref[...])
        o_ref[0, rows] = x_ref[0, rows] + m_ref[0, 5:6, :] * (_rms(y) * g_ref[...])


def _ffn(h, wg, wu, wd, x, mods, mod_off, g_post, *, name):
    grp, n, d = x.shape
    ff = wg.shape[1]
    tm = _tile(n, 1024)
    tok = lambda b, t: (b, t, 0)
    const = lambda b, t: (0, 0)
    resident = dict(index_map=const, pipeline_mode=pl.Buffered(1))
    return pl.pallas_call(
        _ffn_kernel,
        out_shape=jax.ShapeDtypeStruct((grp, n, d), F32),
        grid=(grp, n // tm),
        in_specs=[pl.BlockSpec((1, tm, d), tok),
                  pl.BlockSpec((d, ff), **resident),
                  pl.BlockSpec((d, ff), **resident),
                  pl.BlockSpec((ff, d), **resident),
                  pl.BlockSpec((1, tm, d), tok),
                  pl.BlockSpec((1, N_MOD, d), lambda b, t: (b + mod_off, 0, 0)),
                  pl.BlockSpec((1, d), const)],
        out_specs=pl.BlockSpec((1, tm, d), tok),
        compiler_params=_params("parallel", "parallel"),
        name=name,
    )(h, wg, wu, wd, x, mods, g_post.reshape(1, d))


ROW_ALIGN = 16


def _for_pieces(n, max_piece, fn):
    off = 0
    size = max_piece
    while size >= ROW_ALIGN:
        bit = (n & size) != 0

        @pl.when(bit)
        def _(off=off, size=size):
            fn(pl.multiple_of(off, ROW_ALIGN), size)

        off = off + jnp.where(bit, size, 0)
        size //= 2


def _dispatch_kernel(start_ref, n_ref, loc_ref, zstart_ref, zn_ref, h_ref, pos_t_ref, xs_ref, stage_ref, zero_ref,
                     sem, zsem):
    b = pl.program_id(0)
    last = pl.num_programs(0) - 1
    tm = h_ref.shape[0]
    h = h_ref[...]

    def out_copy(blk, e, off, size):
        start = start_ref[blk * N_EXPERTS + e]
        loc = loc_ref[blk * N_EXPERTS + e]
        src = stage_ref.at[blk % 2, pl.ds(pl.multiple_of(loc + off, ROW_ALIGN), size)]
        dst = xs_ref.at[pl.ds(pl.multiple_of(start + off, ROW_ALIGN), size)]
        return pltpu.make_async_copy(src, dst, sem.at[blk % 2, e])

    def for_chunk_copies(blk, op):
        for e in range(N_EXPERTS):
            _for_pieces(n_ref[blk * N_EXPERTS + e], tm, lambda off, size, e=e: op(out_copy(blk, e, off, size)))

    def zero_copy(start, off, size):
        dst = xs_ref.at[pl.ds(pl.multiple_of(start + off, ROW_ALIGN), size)]
        return pltpu.make_async_copy(zero_ref.at[pl.ds(0, size)], dst, zsem)

    @pl.when(b == 0)
    def _():
        zero_ref[...] = jnp.zeros_like(zero_ref)
        dead_from = zn_ref[N_EXPERTS]
        n_slabs = xs_ref.shape[0] // tm

        def slab(i):
            return zero_copy(pl.multiple_of(i * tm, tm), 0, tm)

        for k in range(N_EXPERTS):
            _for_pieces(zn_ref[k], tm, lambda off, size, k=k: zero_copy(zstart_ref[k], off, size).start())
        lax.fori_loop(dead_from, n_slabs, lambda i, carry: (slab(i).start(), carry)[1], 0)
        for k in range(N_EXPERTS):
            _for_pieces(zn_ref[k], tm, lambda off, size, k=k: zero_copy(zstart_ref[k], off, size).wait())
        lax.fori_loop(dead_from, n_slabs, lambda i, carry: (slab(i).wait(), carry)[1], 0)

    pos1, pos2 = pos_t_ref[0, 0:1, :], pos_t_ref[0, 1:2, :]
    n_stage = stage_ref.shape[1]
    for r0 in range(0, n_stage, SUB_ROWS):
        nr = min(SUB_ROWS, n_stage - r0)
        r = lax.broadcasted_iota(jnp.int32, (nr, tm), 0).astype(F32) + float(r0)
        onehot = jnp.where((r == pos1) | (r == pos2), 1.0, 0.0).astype(BF16)
        stage_ref[b % 2, r0:r0 + nr, :] = _dot(onehot, h).astype(stage_ref.dtype)
    for_chunk_copies(b, lambda cp: cp.start())

    @pl.when(b > 0)
    def _():
        for_chunk_copies(b - 1, lambda cp: cp.wait())

    @pl.when(b == last)
    def _():
        for_chunk_copies(b, lambda cp: cp.wait())


def _dispatch(h, pos_t, starts, counts, locs, zstarts, zcounts, n_rows):
    n, d = h.shape
    nb, _, tm = pos_t.shape
    n_stage = 2 * tm + -(-(ROW_ALIGN - 1) * N_EXPERTS // LANES) * LANES
    return pl.pallas_call(
        _dispatch_kernel,
        out_shape=jax.ShapeDtypeStruct((n_rows, d), BF16),
        grid_spec=pltpu.PrefetchScalarGridSpec(
            num_scalar_prefetch=5,
            grid=(nb,),
            in_specs=[pl.BlockSpec((tm, d), lambda i, *_: (i, 0)),
                      pl.BlockSpec((1, 8, tm), lambda i, *_: (i, 0, 0))],
            out_specs=pl.BlockSpec(memory_space=pl.ANY),
            scratch_shapes=[pltpu.VMEM((2, n_stage, d), BF16), pltpu.VMEM((tm, d), BF16),
                            pltpu.SemaphoreType.DMA((2, N_EXPERTS)), pltpu.SemaphoreType.DMA]),
        compiler_params=_params("arbitrary"),
        name="moe_dispatch",
    )(starts, counts, locs, zstarts, zcounts, h, pos_t)


def _experts_kernel(te_ref, nt_ref, xs_ref, wg_ref, wu_ref, wd_ref, ys_ref, acc_ref):
    i = pl.program_id(0)
    c = pl.program_id(1)

    @pl.when(i < nt_ref[0])
    def _():
        _accumulate_swiglu(c, lambda rows: xs_ref[rows], acc_ref, wg_ref[0], wu_ref[0], wd_ref[0],
                           acc_ref.shape[0], out_ref=ys_ref)

    @pl.when((i >= nt_ref[0]) & (c == 0))
    def _():
        ys_ref[...] = jnp.zeros_like(ys_ref)


def _experts(xs, tile_expert, n_tiles, wg, wu, wd, tmg):
    n_rows, d = xs.shape
    ff = wg.shape[2]
    fc = _tile(ff, 1792)
    total = n_rows // tmg

    def row_map(i, c, te, nt):
        return (jnp.minimum(i, nt[0] - 1), 0)

    def w_in_map(i, c, te, nt):
        return (te[jnp.minimum(i, nt[0] - 1)], 0, jnp.where(i < nt[0], c, ff // fc - 1))

    def w_out_map(i, c, te, nt):
        return (te[jnp.minimum(i, nt[0] - 1)], jnp.where(i < nt[0], c, ff // fc - 1), 0)

    return pl.pallas_call(
        _experts_kernel,
        out_shape=jax.ShapeDtypeStruct((n_rows, d), BF16),
        grid_spec=pltpu.PrefetchScalarGridSpec(
            num_scalar_prefetch=2,
            grid=(total, ff // fc),
            in_specs=[pl.BlockSpec((tmg, d), row_map),
                      pl.BlockSpec((1, d, fc), w_in_map),
                      pl.BlockSpec((1, d, fc), w_in_map),
                      pl.BlockSpec((1, fc, d), w_out_map)],
            out_specs=pl.BlockSpec((tmg, d), lambda i, c, te, nt: (i, 0)),
            scratch_shapes=[pltpu.VMEM((tmg, d), F32)]),
        compiler_params=_params("arbitrary", "arbitrary"),
        name="moe_experts",
    )(tile_expert, n_tiles, xs, wg, wu, wd)


def _combine_kernel(start_ref, n_ref, ys_ref, rank_ref, gate_ref, x_ref, m_ref, g_ref, o_ref, buf_ref, acc_ref, sem):
    per = pl.num_programs(1)
    b = pl.program_id(0) * per + pl.program_id(1)
    tm = x_ref.shape[1]
    half = tm // 2

    n_blocks = pl.num_programs(0) * per

    def in_copy(blk, e, off, size):
        start = start_ref[blk * N_EXPERTS + e]
        src = ys_ref.at[pl.ds(pl.multiple_of(start + off, ROW_ALIGN), size)]
        return pltpu.make_async_copy(src, buf_ref.at[blk % 2, e, pl.ds(off, size)], sem.at[blk % 2, e])

    def fetch(blk):
        for e in range(N_EXPERTS):
            _for_pieces(n_ref[blk * N_EXPERTS + e], tm, lambda off, size, e=e: in_copy(blk, e, off, size).start())

    @pl.when(b == 0)
    def _():
        buf_ref[...] = jnp.zeros_like(buf_ref)
        fetch(b)

    @pl.when(b + 1 < n_blocks)
    def _():
        fetch(b + 1)

    for e in range(N_EXPERTS):
        _for_pieces(n_ref[b * N_EXPERTS + e], tm, lambda off, size, e=e: in_copy(b, e, off, size).wait())

    def gathered(e, hf):
        col = lax.broadcasted_iota(jnp.int32, (tm, half), 1).astype(F32) + float(hf * half)
        onehot = jnp.where(rank_ref[0, :, e:e + 1] == col, 1.0, 0.0).astype(BF16)
        return gate_ref[0, :, e:e + 1] * _dot(onehot, buf_ref[b % 2, e, hf * half:(hf + 1) * half, :])

    acc_ref[...] = functools.reduce(lambda a, c: a + c, [gathered(e, 0) for e in range(N_EXPERTS)])
    for e in range(N_EXPERTS):
        @pl.when(n_ref[b * N_EXPERTS + e] > half)
        def _(e=e):
            acc_ref[...] += gathered(e, 1)
    o_ref[0] = x_ref[0] + m_ref[0, 5:6, :] * (_rms(acc_ref[...]) * g_ref[...])


def _combine(ys, starts, counts, rank, gate, x, mods, g_post, tm):
    b, s, d = x.shape
    per = s // tm
    tok = lambda i, t, *_: (i, t, 0)
    return pl.pallas_call(
        _combine_kernel,
        out_shape=jax.ShapeDtypeStruct((b, s, d), F32),
        grid_spec=pltpu.PrefetchScalarGridSpec(
            num_scalar_prefetch=2,
            grid=(b, per),
            in_specs=[pl.BlockSpec(memory_space=pl.ANY),
                      pl.BlockSpec((1, tm, N_EXPERTS), tok),
                      pl.BlockSpec((1, tm, N_EXPERTS), tok),
                      pl.BlockSpec((1, tm, d), tok),
                      pl.BlockSpec((1, N_MOD, d), lambda i, t, *_: (i, 0, 0)),
                      pl.BlockSpec((1, d), lambda i, t, *_: (0, 0))],
            out_specs=pl.BlockSpec((1, tm, d), tok),
            scratch_shapes=[pltpu.VMEM((2, N_EXPERTS, tm, d), BF16), pltpu.VMEM((tm, d), F32),
                            pltpu.SemaphoreType.DMA((2, N_EXPERTS))]),
        compiler_params=_params("arbitrary", "arbitrary"),
        name="moe_combine",
    )(starts, counts, ys, rank, gate, x, mods, g_post.reshape(1, d))


def _moe(h, rank, gate, pos_t, cnt, x, mods, g_post, wg, wu, wd):
    b, s, d = x.shape
    n = b * s
    nb, _, tm = pos_t.shape
    tmg = _tile(2 * n, 1024)
    total_tiles = -(-(2 * n + (ROW_ALIGN - 1) * N_EXPERTS * nb) // tmg) + N_EXPERTS
    n_rows = total_tiles * tmg
    per_block = cnt[:, 0, :N_EXPERTS].astype(jnp.int32)
    chunk = (per_block + ROW_ALIGN - 1) // ROW_ALIGN * ROW_ALIGN
    rows_e = jnp.sum(chunk, axis=0)
    tiles_per = (rows_e + tmg - 1) // tmg
    tile_end = jnp.cumsum(tiles_per)
    offsets = (tile_end - tiles_per) * tmg
    starts = (offsets[None, :] + jnp.cumsum(chunk, axis=0) - chunk).reshape(nb * N_EXPERTS)
    counts = chunk.reshape(nb * N_EXPERTS)
    locs = (jnp.cumsum(chunk, axis=1) - chunk).reshape(nb * N_EXPERTS)
    zstarts = offsets + rows_e
    zcounts = jnp.concatenate([tiles_per * tmg - rows_e, tile_end[-1:] * (tmg // tm)])
    tile_expert = jnp.sum((tile_end[None, :] <= jnp.arange(total_tiles)[:, None]).astype(jnp.int32), axis=1)
    tile_expert = jnp.minimum(tile_expert, N_EXPERTS - 1)
    n_tiles = tile_end[-1:].astype(jnp.int32)
    xs = _dispatch(h.reshape(n, d), pos_t, starts, counts, locs, zstarts, zcounts.astype(jnp.int32), n_rows)
    ys = _experts(xs, tile_expert, n_tiles, wg, wu, wd, tmg)
    return _combine(ys, starts, counts, rank, gate, x, mods, g_post, tm)


def _even_layouts(w_in, w_out, g_q, g_k, sink):
    dh, half = HEAD_DIM, HEAD_DIM // 2
    within = np.concatenate([np.arange(0, dh, 2), np.arange(1, dh, 2)])
    g = A_Q_HEADS // A_KV_HEADS
    q_order = np.array([h for p in range(g) for h in _pair_heads(p)])
    sizes = [A_Q_HEADS * dh, A_KV_HEADS * dh, A_KV_HEADS * dh, B_Q_HEADS * dh, B_KV_HEADS * dh, B_KV_HEADS * dh]
    base = np.concatenate([[0], np.cumsum(sizes)])

    def head_cols(start, order, permute):
        inner = within if permute else np.arange(dh)
        return np.concatenate([start + h * dh + inner for h in order])

    kv_order = np.arange(A_KV_HEADS)
    cols = np.concatenate([head_cols(base[0], q_order, True), head_cols(base[3], q_order, True),
                           head_cols(base[1], kv_order, True), head_cols(base[4], kv_order, True),
                           head_cols(base[2], kv_order, False), head_cols(base[5], kv_order, False)])
    rows_out = np.concatenate([head_cols(0, q_order, False), head_cols(A_Q_HEADS * dh, q_order, False)])
    gq = jnp.tile(g_q[within], LANES // dh).reshape(1, LANES)
    gk = jnp.tile(g_k[within], LANES // dh).reshape(1, LANES)
    lane = np.arange(LANES)
    head_mean = jnp.asarray((lane[:, None] // dh == lane[None, :] // dh) / dh, BF16)
    return (w_in[:, cols].astype(BF16), w_out[rows_out].astype(BF16), gq, gk, head_mean,
            sink.astype(F32) * LOG2E)


def _rope_tables(n_tokens):
    t = jnp.arange(n_tokens, dtype=jnp.int32)
    row = (t // GRID_W).astype(F32)
    col = (t % GRID_W).astype(F32)
    half = HEAD_DIM // 2
    freqs = ROPE_THETA ** (-jnp.arange(0, half, 2, dtype=F32) / half)
    ang = jnp.concatenate([row[:, None] * freqs, col[:, None] * freqs], axis=-1)
    cos, sin = jnp.cos(ang), jnp.sin(ang)
    reps = LANES // HEAD_DIM
    return (jnp.tile(jnp.concatenate([cos, cos], axis=-1), (1, reps)),
            jnp.tile(jnp.concatenate([-sin, sin], axis=-1), (1, reps)))


def kernel(x, c, ctx, c_ctx, e_w_mod, e_b_mod, e_g_pre_mix, e_g_post_mix, e_g_pre_ffn, e_g_post_ffn, e_w_in, e_w_out, e_g_q, e_g_k, e_sink, e_w_gate, e_w_up, e_w_down, o_w_mod, o_b_mod, o_g_pre_mix, o_g_post_mix, o_g_pre_ffn, o_g_post_ffn, o_w_in, o_w_out, o_rpb, o_w_router, o_w_gate, o_w_up, o_w_down):
    b, s, d = x.shape
    n_ctx = ctx.shape[1]
    rows = s // GRID_W
    ctx = ctx.reshape(1, b * n_ctx, d)

    mods = _mod_vectors(c, c_ctx, e_w_mod[0], e_b_mod[0])
    w_in, w_out, gq, gk, head_mean, sink = _even_layouts(e_w_in[0], e_w_out[0], e_g_q[0], e_g_k[0], e_sink[0])
    even = (gq, gk, head_mean)
    qkv = _inproj(x, mods, 0, e_g_pre_mix[0], w_in, even=even, rope_tabs=_rope_tables(s), name="inproj0_lat")
    ckv = _inproj(ctx, mods, b, e_g_pre_mix[0], w_in, even=even, rope_tabs=(None, None), name="inproj0_ctx")
    ckv = ckv.reshape(b, n_ctx, -1)
    o_lat = _even_attention(qkv, ckv, sink)
    o_ctx = _even_ctx_attention(ckv, sink).reshape(1, b * n_ctx, -1)
    wg, wu, wd = e_w_gate[0].astype(BF16), e_w_up[0].astype(BF16), e_w_down[0].astype(BF16)
    x, h = _outproj(o_lat, w_out, x, mods, 0, e_g_post_mix[0], e_g_pre_ffn[0], name="outproj0_lat")
    x = _ffn(h, wg, wu, wd, x, mods, 0, e_g_post_ffn[0], name="ffn0_lat")
    ctx, h = _outproj(o_ctx, w_out, ctx, mods, b, e_g_post_mix[0], e_g_pre_ffn[0], name="outproj0_ctx")
    ctx = _ffn(h, wg, wu, wd, ctx, mods, b, e_g_post_ffn[0], name="ffn0_ctx")

    mods = _mod_vectors(c, c_ctx, o_w_mod[0], o_b_mod[0])
    hd = C_HEADS * HEAD_DIM
    w_in = o_w_in[0].astype(BF16)
    qkv = _inproj(x, mods, 0, o_g_pre_mix[0], w_in, segments=(("q", hd), ("k", hd), ("v", hd)), name="inproj1_lat")
    w_vk = jnp.concatenate([w_in[:, 2 * hd:], w_in[:, hd:2 * hd]], axis=1)
    ckv = _inproj(ctx, mods, b, o_g_pre_mix[0], w_vk, segments=(("v", hd), ("k", hd)), name="inproj1_ctx")
    ckv = ckv.reshape(b, n_ctx, 3 * hd)
    o_lat = _na_attention(qkv, ckv, _na_bias_table(o_rpb[0], rows), rows)
    wr = jnp.zeros((d, LANES), F32).at[:, :N_EXPERTS].set(o_w_router[0])
    x, h, rank, gate, pos_t, cnt = _outproj(o_lat, o_w_out[0].astype(BF16), x, mods, 0, o_g_post_mix[0],
                                            o_g_pre_ffn[0], router=_split_bf16(wr), name="outproj1_router")
    return _moe(h, rank, gate, pos_t, cnt, x, mods, o_g_post_ffn[0],
                o_w_gate[0].astype(BF16), o_w_up[0].astype(BF16), o_w_down[0].astype(BF16))
```

```python
import functools

import numpy as np
import jax
import jax.numpy as jnp
from jax import lax
from jax.experimental import pallas as pl
from jax.experimental.pallas import tpu as pltpu

GRID_W = 64
HEAD_DIM = 64
A_Q_HEADS = 8
A_KV_HEADS = 2
B_Q_HEADS = 8
B_KV_HEADS = 2
C_HEADS = 16
Q_BLOCK = 128
WINDOW = 128
NA_ROWS = 8
NA_COLS = 16
ROPE_THETA = 10000.0
N_EXPERTS = 8
N_MOD = 6
EPS = 1e-6
NEG_INF = -1e30
ATTN_SCALE = HEAD_DIM ** -0.5
LOG2E = 1.4426950408889634
Q_SCALE = ATTN_SCALE * LOG2E

LANES = 128
VMEM_LIMIT = 56 * 1024 * 1024

F32 = jnp.float32
BF16 = jnp.bfloat16
HIGHEST = lax.Precision.HIGHEST


def _tile(n, pref):
    return pref if n % pref == 0 else n


def _params(*sem):
    return pltpu.CompilerParams(dimension_semantics=sem, vmem_limit_bytes=VMEM_LIMIT)


def _rms(x):
    return x * lax.rsqrt(jnp.mean(x * x, axis=-1, keepdims=True) + EPS)


def _nt_dot(a, b):
    return lax.dot_general(a, b, (((1,), (1,)), ((), ())), preferred_element_type=F32)


def _dot(a, b):
    return jnp.dot(a, b, preferred_element_type=F32)


def _mod_kernel(c_ref, w_ref, b_ref, o_ref):
    c = c_ref[...]
    a = c / (1.0 + jnp.exp(-c))
    o_ref[...] = jnp.dot(a, w_ref[...], precision=HIGHEST, preferred_element_type=F32) + b_ref[...]


def _mod_vectors(c, c_ctx, w_mod, b_mod):
    b, d = c.shape
    rows = -(-(b + 1) // 8) * 8
    cc = jnp.zeros((rows, d), F32).at[:b].set(c).at[b].set(c_ctx)
    n = w_mod.shape[1]
    tn = _tile(n, 1536)
    m = pl.pallas_call(
        _mod_kernel,
        out_shape=jax.ShapeDtypeStruct((rows, n), F32),
        grid=(n // tn,),
        in_specs=[pl.BlockSpec((rows, d), lambda j: (0, 0)),
                  pl.BlockSpec((d, tn), lambda j: (0, j)),
                  pl.BlockSpec((1, tn), lambda j: (0, j))],
        out_specs=pl.BlockSpec((rows, tn), lambda j: (0, j)),
        compiler_params=_params("arbitrary"),
        name="mod_vectors",
    )(cc, w_mod, b_mod.reshape(1, n))
    return m.reshape(rows, N_MOD, d)


def _modulated(x_ref, m_ref, g_ref):
    x = x_ref[0]
    return (_rms(x) * g_ref[...]) * (1.0 + m_ref[0, 1:2, :]) + m_ref[0, 0:1, :]


def _rope_block(y, cos, sin_signed):
    lane = lax.broadcasted_iota(jnp.int32, y.shape, 1)
    partner = jnp.where(lane % HEAD_DIM < HEAD_DIM // 2,
                        pltpu.roll(y, LANES - HEAD_DIM // 2, 1),
                        pltpu.roll(y, HEAD_DIM // 2, 1))
    return y * cos + partner * sin_signed


def _store_value_block(o_ref, col, vb):
    o_ref[0, :, col:col + LANES] = vb.astype(o_ref.dtype)
    o_ref[0, :, col + LANES:col + 2 * LANES] = jnp.ones(vb.shape, o_ref.dtype)


def _inproj_even_kernel(x_ref, m_ref, g_ref, w_ref, gq_ref, gk_ref, hm_ref, cos_ref, sin_ref, o_ref, *, rope):
    h = _modulated(x_ref, m_ref, g_ref).astype(BF16)
    y = _dot(h, w_ref[...])
    n_qa = A_Q_HEADS * HEAD_DIM // LANES
    n_q = n_qa + B_Q_HEADS * HEAD_DIM // LANES
    for i in range(y.shape[1] // LANES):
        yb = y[:, i * LANES:(i + 1) * LANES]
        is_qa, is_q, is_ka, is_k = i < n_qa, i < n_q, i == n_q, n_q <= i < n_q + 2
        if is_qa or is_ka:
            ms = _dot((yb * yb).astype(BF16), hm_ref[...])
            yb = yb * lax.rsqrt(ms + EPS) * (gq_ref[...] if is_qa else gk_ref[...])
        if rope and (is_q or is_k):
            yb = _rope_block(yb, cos_ref[...], sin_ref[...])
        if is_q:
            yb = yb * Q_SCALE
        if is_q or is_k:
            o_ref[0, :, i * LANES:(i + 1) * LANES] = yb.astype(o_ref.dtype)
        else:
            _store_value_block(o_ref, (n_q + 2 + 2 * (i - n_q - 2)) * LANES, yb)


def _inproj_plain_kernel(x_ref, m_ref, g_ref, w_ref, o_ref, *, segments):
    h = _modulated(x_ref, m_ref, g_ref).astype(BF16)
    y = _dot(h, w_ref[...])
    src = dst = 0
    for kind, width in segments:
        seg = y[:, src:src + width]
        if kind == "v":
            for p in range(width // LANES):
                _store_value_block(o_ref, dst + 2 * p * LANES, seg[:, p * LANES:(p + 1) * LANES])
            dst += 2 * width
        else:
            o_ref[0, :, dst:dst + width] = (seg * Q_SCALE if kind == "q" else seg).astype(o_ref.dtype)
            dst += width
        src += width


def _inproj(x, mods, mod_off, g, w, *, even=None, rope_tabs=None, segments=None, name):
    grp, n, d = x.shape
    if even is not None:
        nout = w.shape[1] + (A_KV_HEADS + B_KV_HEADS) * HEAD_DIM
    else:
        nout = sum(width * (2 if kind == "v" else 1) for kind, width in segments)
    tm = _tile(n, 512)
    grid = (grp, n // tm)
    in_specs = [pl.BlockSpec((1, tm, d), lambda b, t: (b, t, 0)),
                pl.BlockSpec((1, N_MOD, d), lambda b, t: (b + mod_off, 0, 0)),
                pl.BlockSpec((1, d), lambda b, t: (0, 0)),
                pl.BlockSpec(w.shape, lambda b, t: (0, 0))]
    args = [x, mods, g.reshape(1, d), w]
    if even is not None:
        gq, gk, hm = even
        cos, sin = rope_tabs
        rope = cos is not None
        if not rope:
            cos = jnp.zeros((tm, LANES), F32)
            sin = cos
            tab_map = lambda b, t: (0, 0)
        else:
            tab_map = lambda b, t: (t, 0)
        in_specs += [pl.BlockSpec((1, LANES), lambda b, t: (0, 0)),
                     pl.BlockSpec((1, LANES), lambda b, t: (0, 0)),
                     pl.BlockSpec((LANES, LANES), lambda b, t: (0, 0)),
                     pl.BlockSpec((tm, LANES), tab_map),
                     pl.BlockSpec((tm, LANES), tab_map)]
        args += [gq, gk, hm, cos, sin]
        body = functools.partial(_inproj_even_kernel, rope=rope)
    else:
        body = functools.partial(_inproj_plain_kernel, segments=segments)
    return pl.pallas_call(
        body,
        out_shape=jax.ShapeDtypeStruct((grp, n, nout), BF16),
        grid=grid,
        in_specs=in_specs,
        out_specs=pl.BlockSpec((1, tm, nout), lambda b, t: (b, t, 0)),
        compiler_params=_params("parallel", "parallel"),
        name=name,
    )(*args)


def _stack_pair(qp):
    lane = lax.broadcasted_iota(jnp.int32, qp.shape, 1)
    zero = jnp.zeros_like(qp)
    return jnp.concatenate([jnp.where(lane < HEAD_DIM, qp, zero),
                            jnp.where(lane >= HEAD_DIM, qp, zero)], axis=0)


def _unstack_pair(o):
    t = o.shape[0] // 2
    lane = lax.broadcasted_iota(jnp.int32, (t, LANES), 1)
    return jnp.where(lane < HEAD_DIM, o[:t], o[t:])


def _softmax_pv(scores, values, sink=None):
    m = functools.reduce(jnp.maximum, [jnp.max(s, axis=-1, keepdims=True) for s in scores])
    if sink is not None:
        m = jnp.maximum(m, sink)
    acc = None
    for s, v in zip(scores, values):
        pv = _dot(jnp.exp2(s - m).astype(v.dtype), v)
        acc = pv if acc is None else acc + pv
    den = acc[:, LANES:]
    if sink is not None:
        den = den + jnp.exp2(sink - m)
    return acc[:, :LANES] * (1.0 / den)


def _pair_sink(sink_ref, h0, h1, t):
    row = lax.broadcasted_iota(jnp.int32, (2 * t, 1), 0)
    return jnp.where(row < t, sink_ref[h0], sink_ref[h1])


def _pair_heads(p):
    return p, p + A_Q_HEADS // A_KV_HEADS


def _even_attn_kernel(sink_ref, q_ref, k_ref, va_ref, vb_ref, ck_ref, cva_ref, cvb_ref, o_ref, *, seq, band):
    j = pl.program_id(1)
    tq = q_ref.shape[1]
    n_pairs = A_Q_HEADS * HEAD_DIM // LANES
    cka, ckb = ck_ref[0, :, 0:LANES], ck_ref[0, :, LANES:2 * LANES]
    cva, cvb = cva_ref[0], cvb_ref[0]
    ka, va = k_ref[0, :, 0:LANES], va_ref[0]
    for p in range(n_pairs):
        qs = _stack_pair(q_ref[0, :, p * LANES:(p + 1) * LANES])
        o = _softmax_pv([_nt_dot(qs, ka), _nt_dot(qs, cka)], [va, cva])
        o_ref[0, :, p * LANES:(p + 1) * LANES] = _unstack_pair(o).astype(o_ref.dtype)
    start = pl.multiple_of(jnp.clip(j * tq - WINDOW, 0, seq - band), Q_BLOCK)
    kb = k_ref[0, pl.ds(start, band), LANES:2 * LANES]
    vb = vb_ref[0, pl.ds(start, band), :]
    qpos = j * tq + lax.broadcasted_iota(jnp.int32, (tq, band), 0)
    kpos = start + lax.broadcasted_iota(jnp.int32, (tq, band), 1)
    ok = jnp.abs(kpos - qpos) <= WINDOW
    ok = jnp.concatenate([ok, ok], axis=0)
    off = n_pairs * LANES
    for p in range(n_pairs):
        h0, h1 = _pair_heads(p)
        qs = _stack_pair(q_ref[0, :, off + p * LANES:off + (p + 1) * LANES])
        s_win = jnp.where(ok, _nt_dot(qs, kb), NEG_INF)
        o = _softmax_pv([s_win, _nt_dot(qs, ckb)], [vb, cvb], _pair_sink(sink_ref, h0, h1, tq))
        o_ref[0, :, off + p * LANES:off + (p + 1) * LANES] = _unstack_pair(o).astype(o_ref.dtype)


def _even_attention(qkv, ckv, sink):
    b, s, _ = qkv.shape
    c = ckv.shape[1]
    tq = _tile(s, 2 * Q_BLOCK)
    nq = 2 * A_Q_HEADS * HEAD_DIM
    w = 2 * LANES
    band = min(tq + 2 * WINDOW, s)
    group = lambda rows, k: pl.BlockSpec((1, rows, w), lambda i, j: (i, 0, nq // w + k))
    return pl.pallas_call(
        functools.partial(_even_attn_kernel, seq=s, band=band),
        out_shape=jax.ShapeDtypeStruct((b, s, nq), BF16),
        grid=(b, s // tq),
        in_specs=[pl.BlockSpec(memory_space=pltpu.SMEM),
                  pl.BlockSpec((1, tq, nq), lambda i, j: (i, j, 0)),
                  group(s, 0), group(s, 1), group(s, 2), group(c, 0), group(c, 1), group(c, 2)],
        out_specs=pl.BlockSpec((1, tq, nq), lambda i, j: (i, j, 0)),
        compiler_params=_params("parallel", "parallel"),
        name="even_attention",
    )(sink, qkv, qkv, qkv, qkv, ckv, ckv, ckv)


def _even_ctx_attn_kernel(sink_ref, q_ref, ck_ref, cva_ref, cvb_ref, o_ref):
    tq = q_ref.shape[1]
    n_pairs = A_Q_HEADS * HEAD_DIM // LANES
    cka, ckb = ck_ref[0, :, 0:LANES], ck_ref[0, :, LANES:2 * LANES]
    cva, cvb = cva_ref[0], cvb_ref[0]
    off = n_pairs * LANES
    for p in range(n_pairs):
        qs = _stack_pair(q_ref[0, :, p * LANES:(p + 1) * LANES])
        o = _softmax_pv([_nt_dot(qs, cka)], [cva])
        o_ref[0, :, p * LANES:(p + 1) * LANES] = _unstack_pair(o).astype(o_ref.dtype)
        h0, h1 = _pair_heads(p)
        qs = _stack_pair(q_ref[0, :, off + p * LANES:off + (p + 1) * LANES])
        o = _softmax_pv([_nt_dot(qs, ckb)], [cvb], _pair_sink(sink_ref, h0, h1, tq))
        o_ref[0, :, off + p * LANES:off + (p + 1) * LANES] = _unstack_pair(o).astype(o_ref.dtype)


def _even_ctx_attention(ckv, sink):
    b, c, _ = ckv.shape
    nq = 2 * A_Q_HEADS * HEAD_DIM
    w = 2 * LANES
    group = lambda k: pl.BlockSpec((1, c, w), lambda i: (i, 0, nq // w + k))
    return pl.pallas_call(
        _even_ctx_attn_kernel,
        out_shape=jax.ShapeDtypeStruct((b, c, nq), BF16),
        grid=(b,),
        in_specs=[pl.BlockSpec(memory_space=pltpu.SMEM),
                  pl.BlockSpec((1, c, nq), lambda i: (i, 0, 0)),
                  group(0), group(1), group(2)],
        out_specs=pl.BlockSpec((1, c, nq), lambda i: (i, 0, 0)),
        compiler_params=_params("parallel"),
        name="even_ctx_attention",
    )(sink, ckv, ckv, ckv, ckv)


def _na_kernel(q_ref, k_ref, v_ref, ck_ref, cv_ref, *rest, rows, kh):
    bias_refs, o_ref = rest[:-1], rest[-1]
    n_loc = kh * GRID_W
    for i, bias_ref in enumerate(bias_refs):
        r = pl.program_id(1) * len(bias_refs) + i
        rs = jnp.clip(r - kh // 2, 0, rows - kh)
        start = pl.multiple_of(rs * GRID_W, GRID_W)
        qrows = slice(i * GRID_W, (i + 1) * GRID_W)
        for p in range(C_HEADS * HEAD_DIM // LANES):
            sl = slice(p * LANES, (p + 1) * LANES)
            vsl = slice(2 * p * LANES, 2 * (p + 1) * LANES)
            qs = _stack_pair(q_ref[0, qrows, sl])
            kb = k_ref[0, pl.ds(start, n_loc), sl]
            vb = v_ref[0, pl.ds(start, n_loc), vsl]
            s_loc = _nt_dot(qs, kb) + bias_ref[0, p]
            o = _softmax_pv([s_loc, _nt_dot(qs, ck_ref[0, :, sl])], [vb, cv_ref[0, :, vsl]])
            o_ref[0, qrows, sl] = _unstack_pair(o).astype(o_ref.dtype)


def _na_bias_table(rpb, rows):
    kh = min(NA_ROWS, rows)
    cols = np.arange(GRID_W)
    col_start = np.clip(cols - NA_COLS // 2, 0, GRID_W - NA_COLS)
    col_in = (cols[None, :] >= col_start[:, None]) & (cols[None, :] < col_start[:, None] + NA_COLS)
    pad = GRID_W - NA_COLS
    padded = jnp.pad(rpb.astype(F32), ((0, 0), (0, 0), (pad, pad)), mode="edge")
    span = padded.shape[-1]
    tiled = jnp.tile(jnp.pad(padded, ((0, 0), (0, 0), (0, 1))), (1, 1, GRID_W))[..., :GRID_W * span]
    t = tiled.reshape(padded.shape[:2] + (GRID_W, span))[..., GRID_W - 1:]
    t = jnp.where(col_in[None, None], t * LOG2E, NEG_INF)
    t = jnp.swapaxes(t, 1, 2)
    bands = [t[:, :, NA_ROWS - 1 - d:NA_ROWS - 1 - d + kh].reshape(C_HEADS // 2, 2 * GRID_W, kh * GRID_W)
             for d in range(kh)]
    return jnp.stack(bands, axis=0)


def _na_attention(qkv, ckv, bias, rows):
    b, s, _ = qkv.shape
    c = ckv.shape[1]
    hd = C_HEADS * HEAD_DIM
    kh = min(NA_ROWS, rows)

    rps = next(k for k in (4, 2, 1) if rows % k == 0)

    def bias_spec(k):
        def bias_map(i, g):
            r = g * rps + k
            return (r - jnp.clip(r - kh // 2, 0, rows - kh), 0, 0, 0)
        return pl.BlockSpec((1,) + bias.shape[1:], bias_map)

    return pl.pallas_call(
        functools.partial(_na_kernel, rows=rows, kh=kh),
        out_shape=jax.ShapeDtypeStruct((b, s, hd), BF16),
        grid=(b, rows // rps),
        in_specs=[pl.BlockSpec((1, rps * GRID_W, hd), lambda i, g: (i, g, 0)),
                  pl.BlockSpec((1, s, hd), lambda i, g: (i, 0, 1)),
                  pl.BlockSpec((1, s, 2 * hd), lambda i, g: (i, 0, 1)),
                  pl.BlockSpec((1, c, hd), lambda i, g: (i, 0, 2)),
                  pl.BlockSpec((1, c, 2 * hd), lambda i, g: (i, 0, 0))] + [bias_spec(k) for k in range(rps)],
        out_specs=pl.BlockSpec((1, rps * GRID_W, hd), lambda i, g: (i, g, 0)),
        compiler_params=_params("parallel", "parallel"),
        name="na_attention",
    )(qkv, qkv, qkv, ckv, ckv, *([bias] * rps))


def _outproj_core(o_ref, w_ref, x_ref, m_ref, gpost_ref, gpre_ref):
    y = _dot(o_ref[0], w_ref[...])
    x = x_ref[0] + m_ref[0, 2:3, :] * (_rms(y) * gpost_ref[...])
    h = (_rms(x) * gpre_ref[...]) * (1.0 + m_ref[0, 4:5, :]) + m_ref[0, 3:4, :]
    return x, h


def _outproj_kernel(o_ref, w_ref, x_ref, m_ref, gpost_ref, gpre_ref, xo_ref, h_ref):
    x, h = _outproj_core(o_ref, w_ref, x_ref, m_ref, gpost_ref, gpre_ref)
    xo_ref[0] = x
    h_ref[0] = h.astype(h_ref.dtype)


def _split_bf16(a):
    hi = a.astype(BF16)
    return hi, (a - hi.astype(F32)).astype(BF16)


def _outproj_router_kernel(o_ref, w_ref, x_ref, m_ref, gpost_ref, gpre_ref, wr_ref,
                           xo_ref, h_ref, rank_ref, gate_ref, pos_t_ref, cnt_ref):
    x, h = _outproj_core(o_ref, w_ref, x_ref, m_ref, gpost_ref, gpre_ref)
    xo_ref[0] = x
    h_ref[0] = h.astype(h_ref.dtype)
    h_hi, h_lo = _split_bf16(h)
    both = _dot(h_hi, wr_ref[...])
    logits = both[:, :LANES] + (both[:, LANES:] + _dot(h_lo, wr_ref[:, :LANES]))
    t = logits.shape[0]
    lane = lax.broadcasted_iota(jnp.int32, logits.shape, 1).astype(F32)
    logits = jnp.where(lane < N_EXPERTS, logits, -jnp.inf)
    v1 = jnp.max(logits, axis=-1, keepdims=True)
    i1 = jnp.min(jnp.where(logits == v1, lane, float(LANES)), axis=-1, keepdims=True)
    rest = jnp.where(lane == i1, -jnp.inf, logits)
    v2 = jnp.max(rest, axis=-1, keepdims=True)
    i2 = jnp.min(jnp.where(rest == v2, lane, float(LANES)), axis=-1, keepdims=True)
    e2 = jnp.exp(v2 - v1)
    w1 = 1.0 / (1.0 + e2)
    w2 = e2 / (1.0 + e2)
    sel1, sel2 = lane == i1, lane == i2
    sel = (sel1 | sel2).astype(F32)
    rr = lax.broadcasted_iota(jnp.int32, (t, t), 0)
    cc = lax.broadcasted_iota(jnp.int32, (t, t), 1)
    before = _dot(jnp.where(rr > cc, 1.0, 0.0).astype(BF16), sel.astype(BF16))
    rank = jnp.where(sel > 0.0, before, -1.0)
    rank_ref[0] = rank[:, :N_EXPERTS]
    gate_ref[0] = jnp.where(sel1, w1, jnp.where(sel2, w2, 0.0))[:, :N_EXPERTS]
    cnt = jnp.sum(sel, axis=0, keepdims=True)
    cnt_ref[0] = cnt
    chunk = jnp.floor((cnt + (ROW_ALIGN - 1)) * (1.0 / ROW_ALIGN)) * ROW_ALIGN
    ei = lax.broadcasted_iota(jnp.int32, (LANES, LANES), 0)
    ej = lax.broadcasted_iota(jnp.int32, (LANES, LANES), 1)
    offs = _dot(jnp.broadcast_to(chunk, (8, LANES)).astype(BF16), jnp.where(ei < ej, 1.0, 0.0).astype(BF16))[0:1]
    local = before + offs
    pos1 = jnp.sum(jnp.where(sel1, local, 0.0), axis=-1, keepdims=True)
    pos2 = jnp.sum(jnp.where(sel2, local, 0.0), axis=-1, keepdims=True)
    pos_t_ref[0] = jnp.where(lane == 0.0, pos1, jnp.where(lane == 1.0, pos2, 0.0)).T[:8, :]


def _outproj(o, w, x, mods, mod_off, g_post, g_pre, *, router=None, name):
    grp, n, d = x.shape
    dm = o.shape[2]
    tm = _tile(n, 512)
    tok = lambda b, t: (b, t, 0)
    const = lambda b, t: (0, 0)
    in_specs = [pl.BlockSpec((1, tm, dm), tok),
                pl.BlockSpec((dm, d), const),
                pl.BlockSpec((1, tm, d), tok),
                pl.BlockSpec((1, N_MOD, d), lambda b, t: (b + mod_off, 0, 0)),
                pl.BlockSpec((1, d), const),
                pl.BlockSpec((1, d), const)]
    args = [o, w, x, mods, g_post.reshape(1, d), g_pre.reshape(1, d)]
    if router is None:
        return pl.pallas_call(
            _outproj_kernel,
            out_shape=(jax.ShapeDtypeStruct((grp, n, d), F32), jax.ShapeDtypeStruct((grp, n, d), BF16)),
            grid=(grp, n // tm),
            in_specs=in_specs,
            out_specs=(pl.BlockSpec((1, tm, d), tok), pl.BlockSpec((1, tm, d), tok)),
            compiler_params=_params("parallel", "parallel"),
            name=name,
        )(*args)
    in_specs += [pl.BlockSpec((d, 2 * LANES), const)]
    args += [jnp.concatenate(router, axis=1)]
    per = n // tm
    blk = lambda b, t: (b * per + t, 0, 0)
    ne = N_EXPERTS
    return pl.pallas_call(
        _outproj_router_kernel,
        out_shape=(jax.ShapeDtypeStruct((grp, n, d), F32), jax.ShapeDtypeStruct((grp, n, d), BF16),
                   jax.ShapeDtypeStruct((grp, n, ne), F32), jax.ShapeDtypeStruct((grp, n, ne), F32),
                   jax.ShapeDtypeStruct((grp * per, 8, tm), F32), jax.ShapeDtypeStruct((grp * per, 1, LANES), F32)),
        grid=(grp, per),
        in_specs=in_specs,
        out_specs=(pl.BlockSpec((1, tm, d), tok), pl.BlockSpec((1, tm, d), tok),
                   pl.BlockSpec((1, tm, ne), tok), pl.BlockSpec((1, tm, ne), tok),
                   pl.BlockSpec((1, 8, tm), blk), pl.BlockSpec((1, 1, LANES), blk)),
        compiler_params=_params("parallel", "parallel"),
        name=name,
    )(*args)


def _swiglu_chunk(h, wg, wu, wd):
    g = _dot(h, wg)
    u = _dot(h, wu)
    a = (g / (1.0 + jnp.exp(-g))) * u
    return _dot(a.astype(BF16), wd)


SUB_ROWS = 256


def _accumulate_swiglu_rows(c, rows, x_ref, acc_ref, wg, wu, wd, out_ref):
    total = jnp.where(c > 0, acc_ref[rows], 0.0) + _swiglu_chunk(x_ref[rows], wg, wu, wd)
    acc_ref[rows] = total
    out_ref[rows] = total.astype(out_ref.dtype)


def _ffn_kernel(h_ref, wg_ref, wu_ref, wd_ref, x_ref, m_ref, g_ref, o_ref):
    n_rows = h_ref.shape[1]
    sub = min(SUB_ROWS, n_rows)
    for r in range(n_rows // sub):
        rows = slice(r * sub, (r + 1) * sub)
        y = _swiglu_chunk(h_ref[0, rows], wg_ref[...], wu_ref[...], wd_ref[...])
        o_ref[0, rows] = x_ref[0, rows] + m_ref[0, 5:6, :] * (_rms(y) * g_ref[...])


def _ffn(h, wg, wu, wd, x, mods, mod_off, g_post, *, name):
    grp, n, d = x.shape
    ff = wg.shape[1]
    tm = _tile(n, 1024)
    tok = lambda b, t: (b, t, 0)
    const = lambda b, t: (0, 0)
    resident = dict(index_map=const, pipeline_mode=pl.Buffered(1))
    return pl.pallas_call(
        _ffn_kernel,
        out_shape=jax.ShapeDtypeStruct((grp, n, d), F32),
        grid=(grp, n // tm),
        in_specs=[pl.BlockSpec((1, tm, d), tok),
                  pl.BlockSpec((d, ff), **resident),
                  pl.BlockSpec((d, ff), **resident),
                  pl.BlockSpec((ff, d), **resident),
                  pl.BlockSpec((1, tm, d), tok),
                  pl.BlockSpec((1, N_MOD, d), lambda b, t: (b + mod_off, 0, 0)),
                  pl.BlockSpec((1, d), const)],
        out_specs=pl.BlockSpec((1, tm, d), tok),
        compiler_params=_params("parallel", "parallel"),
        name=name,
    )(h, wg, wu, wd, x, mods, g_post.reshape(1, d))


ROW_ALIGN = 16


def _for_pieces(n, max_piece, fn):
    off = 0
    size = max_piece
    while size >= ROW_ALIGN:
        bit = (n & size) != 0

        @pl.when(bit)
        def _(off=off, size=size):
            fn(pl.multiple_of(off, ROW_ALIGN), size)

        off = off + jnp.where(bit, size, 0)
        size //= 2


def _dispatch_kernel(start_ref, n_ref, loc_ref, zstart_ref, zn_ref, h_ref, pos_t_ref, xs_ref, stage_ref, zero_ref,
                     sem, zsem):
    b = pl.program_id(0)
    last = pl.num_programs(0) - 1
    tm = h_ref.shape[0]
    h = h_ref[...]

    def out_copy(blk, e, off, size):
        start = start_ref[blk * N_EXPERTS + e]
        loc = loc_ref[blk * N_EXPERTS + e]
        src = stage_ref.at[blk % 2, pl.ds(pl.multiple_of(loc + off, ROW_ALIGN), size)]
        dst = xs_ref.at[pl.ds(pl.multiple_of(start + off, ROW_ALIGN), size)]
        return pltpu.make_async_copy(src, dst, sem.at[blk % 2, e])

    def for_chunk_copies(blk, op):
        for e in range(N_EXPERTS):
            _for_pieces(n_ref[blk * N_EXPERTS + e], tm, lambda off, size, e=e: op(out_copy(blk, e, off, size)))

    def zero_copy(start, off, size):
        dst = xs_ref.at[pl.ds(pl.multiple_of(start + off, ROW_ALIGN), size)]
        return pltpu.make_async_copy(zero_ref.at[pl.ds(0, size)], dst, zsem)

    @pl.when(b == 0)
    def _():
        zero_ref[...] = jnp.zeros_like(zero_ref)
        dead_from = zn_ref[N_EXPERTS]
        n_slabs = xs_ref.shape[0] // tm

        def slab(i):
            return zero_copy(pl.multiple_of(i * tm, tm), 0, tm)

        for k in range(N_EXPERTS):
            _for_pieces(zn_ref[k], tm, lambda off, size, k=k: zero_copy(zstart_ref[k], off, size).start())
        lax.fori_loop(dead_from, n_slabs, lambda i, carry: (slab(i).start(), carry)[1], 0)
        for k in range(N_EXPERTS):
            _for_pieces(zn_ref[k], tm, lambda off, size, k=k: zero_copy(zstart_ref[k], off, size).wait())
        lax.fori_loop(dead_from, n_slabs, lambda i, carry: (slab(i).wait(), carry)[1], 0)

    pos1, pos2 = pos_t_ref[0, 0:1, :], pos_t_ref[0, 1:2, :]
    n_stage = stage_ref.shape[1]
    for r0 in range(0, n_stage, SUB_ROWS):
        nr = min(SUB_ROWS, n_stage - r0)
        r = lax.broadcasted_iota(jnp.int32, (nr, tm), 0).astype(F32) + float(r0)
        onehot = jnp.where((r == pos1) | (r == pos2), 1.0, 0.0).astype(BF16)
        stage_ref[b % 2, r0:r0 + nr, :] = _dot(onehot, h).astype(stage_ref.dtype)
    for_chunk_copies(b, lambda cp: cp.start())

    @pl.when(b > 0)
    def _():
        for_chunk_copies(b - 1, lambda cp: cp.wait())

    @pl.when(b == last)
    def _():
        for_chunk_copies(b, lambda cp: cp.wait())


def _dispatch(h, pos_t, starts, counts, locs, zstarts, zcounts, n_rows):
    n, d = h.shape
    nb, _, tm = pos_t.shape
    n_stage = 2 * tm + -(-(ROW_ALIGN - 1) * N_EXPERTS // LANES) * LANES
    return pl.pallas_call(
        _dispatch_kernel,
        out_shape=jax.ShapeDtypeStruct((n_rows, d), BF16),
        grid_spec=pltpu.PrefetchScalarGridSpec(
            num_scalar_prefetch=5,
            grid=(nb,),
            in_specs=[pl.BlockSpec((tm, d), lambda i, *_: (i, 0)),
                      pl.BlockSpec((1, 8, tm), lambda i, *_: (i, 0, 0))],
            out_specs=pl.BlockSpec(memory_space=pl.ANY),
            scratch_shapes=[pltpu.VMEM((2, n_stage, d), BF16), pltpu.VMEM((tm, d), BF16),
                            pltpu.SemaphoreType.DMA((2, N_EXPERTS)), pltpu.SemaphoreType.DMA]),
        compiler_params=_params("arbitrary"),
        name="moe_dispatch",
    )(starts, counts, locs, zstarts, zcounts, h, pos_t)


def _experts_kernel(te_ref, nt_ref, used_ref, xs_ref, wg_ref, wu_ref, wd_ref, ys_ref, acc_ref):
    c = pl.program_id(1)
    used = used_ref[pl.program_id(0)]
    tmg = acc_ref.shape[0]
    sub = min(SUB_ROWS, tmg)
    subs = [slice(r * sub, (r + 1) * sub) for r in range(tmg // sub)]

    def run(rows):
        _accumulate_swiglu_rows(c, rows, xs_ref, acc_ref, wg_ref[0], wu_ref[0], wd_ref[0], ys_ref)

    @pl.when(used == tmg)
    def _():
        for rows in subs:
            run(rows)

    @pl.when(used < tmg)
    def _():
        for rows in subs:
            pl.when(used > rows.start)(functools.partial(run, rows))

            @pl.when((used <= rows.start) & (c == 0))
            def _(rows=rows):
                ys_ref[rows] = jnp.zeros((sub, ys_ref.shape[1]), ys_ref.dtype)


def _experts(xs, tile_expert, n_tiles, tile_used, wg, wu, wd, tmg):
    n_rows, d = xs.shape
    ff = wg.shape[2]
    fc = _tile(ff, 1792)
    total = n_rows // tmg

    def row_map(i, c, te, nt, used):
        return (jnp.minimum(i, nt[0] - 1), 0)

    def w_in_map(i, c, te, nt, used):
        return (te[jnp.minimum(i, nt[0] - 1)], 0, jnp.where(i < nt[0], c, ff // fc - 1))

    def w_out_map(i, c, te, nt, used):
        return (te[jnp.minimum(i, nt[0] - 1)], jnp.where(i < nt[0], c, ff // fc - 1), 0)

    return pl.pallas_call(
        _experts_kernel,
        out_shape=jax.ShapeDtypeStruct((n_rows, d), BF16),
        grid_spec=pltpu.PrefetchScalarGridSpec(
            num_scalar_prefetch=3,
            grid=(total, ff // fc),
            in_specs=[pl.BlockSpec((tmg, d), row_map),
                      pl.BlockSpec((1, d, fc), w_in_map),
                      pl.BlockSpec((1, d, fc), w_in_map),
                      pl.BlockSpec((1, fc, d), w_out_map)],
            out_specs=pl.BlockSpec((tmg, d), lambda i, c, te, nt, used: (i, 0)),
            scratch_shapes=[pltpu.VMEM((tmg, d), F32)]),
        compiler_params=_params("arbitrary", "arbitrary"),
        name="moe_experts",
    )(tile_expert, n_tiles, tile_used, xs, wg, wu, wd)


def _combine_kernel(start_ref, n_ref, ys_ref, rank_ref, gate_ref, x_ref, m_ref, g_ref, o_ref, buf_ref, acc_ref, sem):
    per = pl.num_programs(1)
    b = pl.program_id(0) * per + pl.program_id(1)
    tm = x_ref.shape[1]
    half = tm // 2

    n_blocks = pl.num_programs(0) * per

    def in_copy(blk, e, off, size):
        start = start_ref[blk * N_EXPERTS + e]
        src = ys_ref.at[pl.ds(pl.multiple_of(start + off, ROW_ALIGN), size)]
        return pltpu.make_async_copy(src, buf_ref.at[blk % 2, e, pl.ds(off, size)], sem.at[blk % 2, e])

    def fetch(blk):
        for e in range(N_EXPERTS):
            _for_pieces(n_ref[blk * N_EXPERTS + e], tm, lambda off, size, e=e: in_copy(blk, e, off, size).start())

    @pl.when(b == 0)
    def _():
        buf_ref[...] = jnp.zeros_like(buf_ref)
        fetch(b)

    @pl.when(b + 1 < n_blocks)
    def _():
        fetch(b + 1)

    for e in range(N_EXPERTS):
        _for_pieces(n_ref[b * N_EXPERTS + e], tm, lambda off, size, e=e: in_copy(b, e, off, size).wait())

    def gathered(e, hf):
        col = lax.broadcasted_iota(jnp.int32, (tm, half), 1).astype(F32) + float(hf * half)
        onehot = jnp.where(rank_ref[0, :, e:e + 1] == col, 1.0, 0.0).astype(BF16)
        return gate_ref[0, :, e:e + 1] * _dot(onehot, buf_ref[b % 2, e, hf * half:(hf + 1) * half, :])

    acc_ref[...] = functools.reduce(lambda a, c: a + c, [gathered(e, 0) for e in range(N_EXPERTS)])
    for e in range(N_EXPERTS):
        @pl.when(n_ref[b * N_EXPERTS + e] > half)
        def _(e=e):
            acc_ref[...] += gathered(e, 1)
    o_ref[0] = x_ref[0] + m_ref[0, 5:6, :] * (_rms(acc_ref[...]) * g_ref[...])


def _combine(ys, starts, counts, rank, gate, x, mods, g_post, tm):
    b, s, d = x.shape
    per = s // tm
    tok = lambda i, t, *_: (i, t, 0)
    return pl.pallas_call(
        _combine_kernel,
        out_shape=jax.ShapeDtypeStruct((b, s, d), F32),
        grid_spec=pltpu.PrefetchScalarGridSpec(
            num_scalar_prefetch=2,
            grid=(b, per),
            in_specs=[pl.BlockSpec(memory_space=pl.ANY),
                      pl.BlockSpec((1, tm, N_EXPERTS), tok),
                      pl.BlockSpec((1, tm, N_EXPERTS), tok),
                      pl.BlockSpec((1, tm, d), tok),
                      pl.BlockSpec((1, N_MOD, d), lambda i, t, *_: (i, 0, 0)),
                      pl.BlockSpec((1, d), lambda i, t, *_: (0, 0))],
            out_specs=pl.BlockSpec((1, tm, d), tok),
            scratch_shapes=[pltpu.VMEM((2, N_EXPERTS, tm, d), BF16), pltpu.VMEM((tm, d), F32),
                            pltpu.SemaphoreType.DMA((2, N_EXPERTS))]),
        compiler_params=_params("arbitrary", "arbitrary"),
        name="moe_combine",
    )(starts, counts, ys, rank, gate, x, mods, g_post.reshape(1, d))


def _moe(h, rank, gate, pos_t, cnt, x, mods, g_post, wg, wu, wd):
    b, s, d = x.shape
    n = b * s
    nb, _, tm = pos_t.shape
    tmg = _tile(2 * n, 1024)
    total_tiles = -(-(2 * n + (ROW_ALIGN - 1) * N_EXPERTS * nb) // tmg) + N_EXPERTS
    n_rows = total_tiles * tmg
    per_block = cnt[:, 0, :N_EXPERTS].astype(jnp.int32)
    chunk = (per_block + ROW_ALIGN - 1) // ROW_ALIGN * ROW_ALIGN
    rows_e = jnp.sum(chunk, axis=0)
    tiles_per = (rows_e + tmg - 1) // tmg
    tile_end = jnp.cumsum(tiles_per)
    offsets = (tile_end - tiles_per) * tmg
    starts = (offsets[None, :] + jnp.cumsum(chunk, axis=0) - chunk).reshape(nb * N_EXPERTS)
    counts = chunk.reshape(nb * N_EXPERTS)
    locs = (jnp.cumsum(chunk, axis=1) - chunk).reshape(nb * N_EXPERTS)
    zstarts = offsets + rows_e
    zcounts = jnp.concatenate([tiles_per * tmg - rows_e, tile_end[-1:] * (tmg // tm)])
    tile_expert = jnp.sum((tile_end[None, :] <= jnp.arange(total_tiles)[:, None]).astype(jnp.int32), axis=1)
    tile_expert = jnp.minimum(tile_expert, N_EXPERTS - 1)
    n_tiles = tile_end[-1:].astype(jnp.int32)
    tile_index = jnp.arange(total_tiles) - (tile_end - tiles_per)[tile_expert]
    tile_used = jnp.clip(rows_e[tile_expert] - tile_index * tmg, 0, tmg).astype(jnp.int32)
    xs = _dispatch(h.reshape(n, d), pos_t, starts, counts, locs, zstarts, zcounts.astype(jnp.int32), n_rows)
    ys = _experts(xs, tile_expert, n_tiles, tile_used, wg, wu, wd, tmg)
    return _combine(ys, starts, counts, rank, gate, x, mods, g_post, tm)


def _even_layouts(w_in, w_out, g_q, g_k, sink):
    dh, half = HEAD_DIM, HEAD_DIM // 2
    within = np.concatenate([np.arange(0, dh, 2), np.arange(1, dh, 2)])
    g = A_Q_HEADS // A_KV_HEADS
    q_order = np.array([h for p in range(g) for h in _pair_heads(p)])
    sizes = [A_Q_HEADS * dh, A_KV_HEADS * dh, A_KV_HEADS * dh, B_Q_HEADS * dh, B_KV_HEADS * dh, B_KV_HEADS * dh]
    base = np.concatenate([[0], np.cumsum(sizes)])

    def head_cols(start, order, permute):
        inner = within if permute else np.arange(dh)
        return np.concatenate([start + h * dh + inner for h in order])

    kv_order = np.arange(A_KV_HEADS)
    cols = np.concatenate([head_cols(base[0], q_order, True), head_cols(base[3], q_order, True),
                           head_cols(base[1], kv_order, True), head_cols(base[4], kv_order, True),
                           head_cols(base[2], kv_order, False), head_cols(base[5], kv_order, False)])
    rows_out = np.concatenate([head_cols(0, q_order, False), head_cols(A_Q_HEADS * dh, q_order, False)])
    gq = jnp.tile(g_q[within], LANES // dh).reshape(1, LANES)
    gk = jnp.tile(g_k[within], LANES // dh).reshape(1, LANES)
    lane = np.arange(LANES)
    head_mean = jnp.asarray((lane[:, None] // dh == lane[None, :] // dh) / dh, BF16)
    return (w_in[:, cols].astype(BF16), w_out[rows_out].astype(BF16), gq, gk, head_mean,
            sink.astype(F32) * LOG2E)


def _rope_tables(n_tokens):
    t = jnp.arange(n_tokens, dtype=jnp.int32)
    row = (t // GRID_W).astype(F32)
    col = (t % GRID_W).astype(F32)
    half = HEAD_DIM // 2
    freqs = ROPE_THETA ** (-jnp.arange(0, half, 2, dtype=F32) / half)
    ang = jnp.concatenate([row[:, None] * freqs, col[:, None] * freqs], axis=-1)
    cos, sin = jnp.cos(ang), jnp.sin(ang)
    reps = LANES // HEAD_DIM
    return (jnp.tile(jnp.concatenate([cos, cos], axis=-1), (1, reps)),
            jnp.tile(jnp.concatenate([-sin, sin], axis=-1), (1, reps)))


def kernel(x, c, ctx, c_ctx, e_w_mod, e_b_mod, e_g_pre_mix, e_g_post_mix, e_g_pre_ffn, e_g_post_ffn, e_w_in, e_w_out, e_g_q, e_g_k, e_sink, e_w_gate, e_w_up, e_w_down, o_w_mod, o_b_mod, o_g_pre_mix, o_g_post_mix, o_g_pre_ffn, o_g_post_ffn, o_w_in, o_w_out, o_rpb, o_w_router, o_w_gate, o_w_up, o_w_down):
    b, s, d = x.shape
    n_ctx = ctx.shape[1]
    rows = s // GRID_W
    ctx = ctx.reshape(1, b * n_ctx, d)

    mods = _mod_vectors(c, c_ctx, e_w_mod[0], e_b_mod[0])
    w_in, w_out, gq, gk, head_mean, sink = _even_layouts(e_w_in[0], e_w_out[0], e_g_q[0], e_g_k[0], e_sink[0])
    even = (gq, gk, head_mean)
    qkv = _inproj(x, mods, 0, e_g_pre_mix[0], w_in, even=even, rope_tabs=_rope_tables(s), name="inproj0_lat")
    ckv = _inproj(ctx, mods, b, e_g_pre_mix[0], w_in, even=even, rope_tabs=(None, None), name="inproj0_ctx")
    ckv = ckv.reshape(b, n_ctx, -1)
    o_lat = _even_attention(qkv, ckv, sink)
    o_ctx = _even_ctx_attention(ckv, sink).reshape(1, b * n_ctx, -1)
    wg, wu, wd = e_w_gate[0].astype(BF16), e_w_up[0].astype(BF16), e_w_down[0].astype(BF16)
    x, h = _outproj(o_lat, w_out, x, mods, 0, e_g_post_mix[0], e_g_pre_ffn[0], name="outproj0_lat")
    x = _ffn(h, wg, wu, wd, x, mods, 0, e_g_post_ffn[0], name="ffn0_lat")
    ctx, h = _outproj(o_ctx, w_out, ctx, mods, b, e_g_post_mix[0], e_g_pre_ffn[0], name="outproj0_ctx")
    ctx = _ffn(h, wg, wu, wd, ctx, mods, b, e_g_post_ffn[0], name="ffn0_ctx")

    mods = _mod_vectors(c, c_ctx, o_w_mod[0], o_b_mod[0])
    hd = C_HEADS * HEAD_DIM
    w_in = o_w_in[0].astype(BF16)
    qkv = _inproj(x, mods, 0, o_g_pre_mix[0], w_in, segments=(("q", hd), ("k", hd), ("v", hd)), name="inproj1_lat")
    w_vk = jnp.concatenate([w_in[:, 2 * hd:], w_in[:, hd:2 * hd]], axis=1)
    ckv = _inproj(ctx, mods, b, o_g_pre_mix[0], w_vk, segments=(("v", hd), ("k", hd)), name="inproj1_ctx")
    ckv = ckv.reshape(b, n_ctx, 3 * hd)
    o_lat = _na_attention(qkv, ckv, _na_bias_table(o_rpb[0], rows), rows)
    wr = jnp.zeros((d, LANES), F32).at[:, :N_EXPERTS].set(o_w_router[0])
    x, h, rank, gate, pos_t, cnt = _outproj(o_lat, o_w_out[0].astype(BF16), x, mods, 0, o_g_post_mix[0],
                                            o_g_pre_ffn[0], router=_split_bf16(wr), name="outproj1_router")
    return _moe(h, rank, gate, pos_t, cnt, x, mods, o_g_post_ffn[0],
                o_w_gate[0].astype(BF16), o_w_up[0].astype(BF16), o_w_down[0].astype(BF16))
```

```python
import functools

import numpy as np
import jax
import jax.numpy as jnp
from jax import lax
from jax.experimental import pallas as pl
from jax.experimental.pallas import tpu as pltpu

GRID_W = 64
HEAD_DIM = 64
A_Q_HEADS = 8
A_KV_HEADS = 2
B_Q_HEADS = 8
B_KV_HEADS = 2
C_HEADS = 16
Q_BLOCK = 128
WINDOW = 128
NA_ROWS = 8
NA_COLS = 16
ROPE_THETA = 10000.0
N_EXPERTS = 8
N_MOD = 6
EPS = 1e-6
NEG_INF = -1e30
ATTN_SCALE = HEAD_DIM ** -0.5
LOG2E = 1.4426950408889634
Q_SCALE = ATTN_SCALE * LOG2E

LANES = 128
VMEM_LIMIT = 56 * 1024 * 1024

F32 = jnp.float32
BF16 = jnp.bfloat16
HIGHEST = lax.Precision.HIGHEST


def _tile(n, pref):
    return pref if n % pref == 0 else n


def _params(*sem):
    return pltpu.CompilerParams(dimension_semantics=sem, vmem_limit_bytes=VMEM_LIMIT)


def _rms_scaled(x, scale):
    return (x * lax.rsqrt(jnp.mean(x * x, axis=-1, keepdims=True) + EPS)) * scale


def _nt_dot(a, b):
    return lax.dot_general(a, b, (((1,), (1,)), ((), ())), preferred_element_type=F32)


def _dot(a, b):
    return jnp.dot(a, b, preferred_element_type=F32)


def _mod_kernel(c_ref, w_ref, b_ref, o_ref):
    c = c_ref[...]
    a = c / (1.0 + jnp.exp(-c))
    o_ref[...] = jnp.dot(a, w_ref[...], precision=HIGHEST, preferred_element_type=F32) + b_ref[...]


def _mod_vectors(c, c_ctx, w_mod, b_mod):
    b, d = c.shape
    rows = -(-(b + 1) // 8) * 8
    cc = jnp.zeros((rows, d), F32).at[:b].set(c).at[b].set(c_ctx)
    n = w_mod.shape[1]
    tn = _tile(n, 1536)
    m = pl.pallas_call(
        _mod_kernel,
        out_shape=jax.ShapeDtypeStruct((rows, n), F32),
        grid=(n // tn,),
        in_specs=[pl.BlockSpec((rows, d), lambda j: (0, 0)),
                  pl.BlockSpec((d, tn), lambda j: (0, j)),
                  pl.BlockSpec((1, tn), lambda j: (0, j))],
        out_specs=pl.BlockSpec((rows, tn), lambda j: (0, j)),
        compiler_params=_params("arbitrary"),
        name="mod_vectors",
    )(cc, w_mod, b_mod.reshape(1, n))
    return m.reshape(rows, N_MOD, d)


def _modulated(x_ref, m_ref, g_ref):
    x = x_ref[0]
    return _rms_scaled(x, g_ref[...] * (1.0 + m_ref[0, 1:2, :])) + m_ref[0, 0:1, :]


def _rope_block(y, cos, sin_signed):
    lane = lax.broadcasted_iota(jnp.int32, y.shape, 1)
    partner = jnp.where(lane % HEAD_DIM < HEAD_DIM // 2,
                        pltpu.roll(y, LANES - HEAD_DIM // 2, 1),
                        pltpu.roll(y, HEAD_DIM // 2, 1))
    return y * cos + partner * sin_signed


def _store_value_block(o_ref, col, vb):
    o_ref[0, :, col:col + LANES] = vb.astype(o_ref.dtype)
    o_ref[0, :, col + LANES:col + 2 * LANES] = jnp.ones(vb.shape, o_ref.dtype)


def _inproj_even_kernel(x_ref, m_ref, g_ref, w_ref, gq_ref, gk_ref, hm_ref, cos_ref, sin_ref, o_ref, *, rope):
    h = _modulated(x_ref, m_ref, g_ref).astype(BF16)
    y = _dot(h, w_ref[...])
    n_qa = A_Q_HEADS * HEAD_DIM // LANES
    n_q = n_qa + B_Q_HEADS * HEAD_DIM // LANES
    for i in range(y.shape[1] // LANES):
        yb = y[:, i * LANES:(i + 1) * LANES]
        is_qa, is_q, is_ka, is_k = i < n_qa, i < n_q, i == n_q, n_q <= i < n_q + 2
        if is_qa or is_ka:
            ms = _dot((yb * yb).astype(BF16), hm_ref[...])
            yb = yb * lax.rsqrt(ms + EPS) * (gq_ref[...] if is_qa else gk_ref[...])
        if rope and is_q:
            yb = _rope_block(yb, cos_ref[...] * Q_SCALE, sin_ref[...] * Q_SCALE)
        elif rope and is_k:
            yb = _rope_block(yb, cos_ref[...], sin_ref[...])
        elif is_q:
            yb = yb * Q_SCALE
        if is_q or is_k:
            o_ref[0, :, i * LANES:(i + 1) * LANES] = yb.astype(o_ref.dtype)
        else:
            _store_value_block(o_ref, (n_q + 2 + 2 * (i - n_q - 2)) * LANES, yb)


def _inproj_plain_kernel(x_ref, m_ref, g_ref, w_ref, o_ref, *, segments):
    h = _modulated(x_ref, m_ref, g_ref).astype(BF16)
    y = _dot(h, w_ref[...])
    src = dst = 0
    for kind, width in segments:
        seg = y[:, src:src + width]
        if kind == "v":
            for p in range(width // LANES):
                _store_value_block(o_ref, dst + 2 * p * LANES, seg[:, p * LANES:(p + 1) * LANES])
            dst += 2 * width
        else:
            o_ref[0, :, dst:dst + width] = (seg * Q_SCALE if kind == "q" else seg).astype(o_ref.dtype)
            dst += width
        src += width


def _inproj(x, mods, mod_off, g, w, *, even=None, rope_tabs=None, segments=None, name):
    grp, n, d = x.shape
    if even is not None:
        nout = w.shape[1] + (A_KV_HEADS + B_KV_HEADS) * HEAD_DIM
    else:
        nout = sum(width * (2 if kind == "v" else 1) for kind, width in segments)
    tm = _tile(n, 1024)
    grid = (grp, n // tm)
    in_specs = [pl.BlockSpec((1, tm, d), lambda b, t: (b, t, 0)),
                pl.BlockSpec((1, N_MOD, d), lambda b, t: (b + mod_off, 0, 0)),
                pl.BlockSpec((1, d), lambda b, t: (0, 0)),
                pl.BlockSpec(w.shape, lambda b, t: (0, 0))]
    args = [x, mods, g.reshape(1, d), w]
    if even is not None:
        gq, gk, hm = even
        cos, sin = rope_tabs
        rope = cos is not None
        if not rope:
            cos = jnp.zeros((tm, LANES), F32)
            sin = cos
            tab_map = lambda b, t: (0, 0)
        else:
            tab_map = lambda b, t: (t, 0)
        in_specs += [pl.BlockSpec((1, LANES), lambda b, t: (0, 0)),
                     pl.BlockSpec((1, LANES), lambda b, t: (0, 0)),
                     pl.BlockSpec((LANES, LANES), lambda b, t: (0, 0)),
                     pl.BlockSpec((tm, LANES), tab_map),
                     pl.BlockSpec((tm, LANES), tab_map)]
        args += [gq, gk, hm, cos, sin]
        body = functools.partial(_inproj_even_kernel, rope=rope)
    else:
        body = functools.partial(_inproj_plain_kernel, segments=segments)
    return pl.pallas_call(
        body,
        out_shape=jax.ShapeDtypeStruct((grp, n, nout), BF16),
        grid=grid,
        in_specs=in_specs,
        out_specs=pl.BlockSpec((1, tm, nout), lambda b, t: (b, t, 0)),
        compiler_params=_params("parallel", "parallel"),
        name=name,
    )(*args)


def _stack_pair(qp):
    lane = lax.broadcasted_iota(jnp.int32, qp.shape, 1)
    zero = jnp.zeros_like(qp)
    return jnp.concatenate([jnp.where(lane < HEAD_DIM, qp, zero),
                            jnp.where(lane >= HEAD_DIM, qp, zero)], axis=0)


def _unstack_pair(o):
    t = o.shape[0] // 2
    lane = lax.broadcasted_iota(jnp.int32, (t, LANES), 1)
    return jnp.where(lane < HEAD_DIM, o[:t], o[t:])


def _softmax_pv(scores, values, sink=None):
    m = functools.reduce(jnp.maximum, [jnp.max(s, axis=-1, keepdims=True) for s in scores])
    if sink is not None:
        m = jnp.maximum(m, sink)
    acc = None
    for s, v in zip(scores, values):
        pv = _dot(jnp.exp2(s - m).astype(v.dtype), v)
        acc = pv if acc is None else acc + pv
    den = acc[:, LANES:]
    if sink is not None:
        den = den + jnp.exp2(sink - m)
    return acc[:, :LANES] * (1.0 / den)


def _pair_sink(sink_ref, h0, h1, t):
    row = lax.broadcasted_iota(jnp.int32, (2 * t, 1), 0)
    return jnp.where(row < t, sink_ref[h0], sink_ref[h1])


def _pair_heads(p):
    return p, p + A_Q_HEADS // A_KV_HEADS


def _even_attn_kernel(sink_ref, q_ref, k_ref, va_ref, vb_ref, ck_ref, cva_ref, cvb_ref, o_ref, *, seq, band):
    j = pl.program_id(1)
    tq = q_ref.shape[1]
    n_pairs = A_Q_HEADS * HEAD_DIM // LANES
    cka, ckb = ck_ref[0, :, 0:LANES], ck_ref[0, :, LANES:2 * LANES]
    cva, cvb = cva_ref[0], cvb_ref[0]
    ka, va = k_ref[0, :, 0:LANES], va_ref[0]
    for p in range(n_pairs):
        qs = _stack_pair(q_ref[0, :, p * LANES:(p + 1) * LANES])
        o = _softmax_pv([_nt_dot(qs, ka), _nt_dot(qs, cka)], [va, cva])
        o_ref[0, :, p * LANES:(p + 1) * LANES] = _unstack_pair(o).astype(o_ref.dtype)
    start = pl.multiple_of(jnp.clip(j * tq - WINDOW, 0, seq - band), Q_BLOCK)
    kb = k_ref[0, pl.ds(start, band), LANES:2 * LANES]
    vb = vb_ref[0, pl.ds(start, band), :]
    qpos = j * tq + lax.broadcasted_iota(jnp.int32, (tq, band), 0)
    kpos = start + lax.broadcasted_iota(jnp.int32, (tq, band), 1)
    ok = jnp.abs(kpos - qpos) <= WINDOW
    ok = jnp.concatenate([ok, ok], axis=0)
    off = n_pairs * LANES
    for p in range(n_pairs):
        h0, h1 = _pair_heads(p)
        qs = _stack_pair(q_ref[0, :, off + p * LANES:off + (p + 1) * LANES])
        s_win = jnp.where(ok, _nt_dot(qs, kb), NEG_INF)
        o = _softmax_pv([s_win, _nt_dot(qs, ckb)], [vb, cvb], _pair_sink(sink_ref, h0, h1, tq))
        o_ref[0, :, off + p * LANES:off + (p + 1) * LANES] = _unstack_pair(o).astype(o_ref.dtype)


def _even_attention(qkv, ckv, sink):
    b, s, _ = qkv.shape
    c = ckv.shape[1]
    tq = _tile(s, 2 * Q_BLOCK)
    nq = 2 * A_Q_HEADS * HEAD_DIM
    w = 2 * LANES
    band = min(tq + 2 * WINDOW, s)
    group = lambda rows, k: pl.BlockSpec((1, rows, w), lambda i, j: (i, 0, nq // w + k))
    return pl.pallas_call(
        functools.partial(_even_attn_kernel, seq=s, band=band),
        out_shape=jax.ShapeDtypeStruct((b, s, nq), BF16),
        grid=(b, s // tq),
        in_specs=[pl.BlockSpec(memory_space=pltpu.SMEM),
                  pl.BlockSpec((1, tq, nq), lambda i, j: (i, j, 0)),
                  group(s, 0), group(s, 1), group(s, 2), group(c, 0), group(c, 1), group(c, 2)],
        out_specs=pl.BlockSpec((1, tq, nq), lambda i, j: (i, j, 0)),
        compiler_params=_params("parallel", "parallel"),
        name="even_attention",
    )(sink, qkv, qkv, qkv, qkv, ckv, ckv, ckv)


def _even_ctx_attn_kernel(sink_ref, q_ref, ck_ref, cva_ref, cvb_ref, o_ref):
    tq = q_ref.shape[1]
    n_pairs = A_Q_HEADS * HEAD_DIM // LANES
    cka, ckb = ck_ref[0, :, 0:LANES], ck_ref[0, :, LANES:2 * LANES]
    cva, cvb = cva_ref[0], cvb_ref[0]
    off = n_pairs * LANES
    for p in range(n_pairs):
        qs = _stack_pair(q_ref[0, :, p * LANES:(p + 1) * LANES])
        o = _softmax_pv([_nt_dot(qs, cka)], [cva])
        o_ref[0, :, p * LANES:(p + 1) * LANES] = _unstack_pair(o).astype(o_ref.dtype)
        h0, h1 = _pair_heads(p)
        qs = _stack_pair(q_ref[0, :, off + p * LANES:off + (p + 1) * LANES])
        o = _softmax_pv([_nt_dot(qs, ckb)], [cvb], _pair_sink(sink_ref, h0, h1, tq))
        o_ref[0, :, off + p * LANES:off + (p + 1) * LANES] = _unstack_pair(o).astype(o_ref.dtype)


def _even_ctx_attention(ckv, sink):
    b, c, _ = ckv.shape
    nq = 2 * A_Q_HEADS * HEAD_DIM
    w = 2 * LANES
    group = lambda k: pl.BlockSpec((1, c, w), lambda i: (i, 0, nq // w + k))
    return pl.pallas_call(
        _even_ctx_attn_kernel,
        out_shape=jax.ShapeDtypeStruct((b, c, nq), BF16),
        grid=(b,),
        in_specs=[pl.BlockSpec(memory_space=pltpu.SMEM),
                  pl.BlockSpec((1, c, nq), lambda i: (i, 0, 0)),
                  group(0), group(1), group(2)],
        out_specs=pl.BlockSpec((1, c, nq), lambda i: (i, 0, 0)),
        compiler_params=_params("parallel"),
        name="even_ctx_attention",
    )(sink, ckv, ckv, ckv, ckv)


def _na_kernel(q_ref, k_ref, v_ref, ck_ref, cv_ref, *rest, rows, kh):
    bias_refs, o_ref = rest[:-1], rest[-1]
    n_loc = kh * GRID_W
    for i, bias_ref in enumerate(bias_refs):
        r = pl.program_id(1) * len(bias_refs) + i
        rs = jnp.clip(r - kh // 2, 0, rows - kh)
        start = pl.multiple_of(rs * GRID_W, GRID_W)
        qrows = slice(i * GRID_W, (i + 1) * GRID_W)
        for p in range(C_HEADS * HEAD_DIM // LANES):
            sl = slice(p * LANES, (p + 1) * LANES)
            vsl = slice(2 * p * LANES, 2 * (p + 1) * LANES)
            qs = _stack_pair(q_ref[0, qrows, sl])
            kb = k_ref[0, pl.ds(start, n_loc), sl]
            vb = v_ref[0, pl.ds(start, n_loc), vsl]
            s_loc = _nt_dot(qs, kb) + bias_ref[0, p]
            o = _softmax_pv([s_loc, _nt_dot(qs, ck_ref[0, :, sl])], [vb, cv_ref[0, :, vsl]])
            o_ref[0, qrows, sl] = _unstack_pair(o).astype(o_ref.dtype)


def _na_bias_table(rpb, rows):
    kh = min(NA_ROWS, rows)
    cols = np.arange(GRID_W)
    col_start = np.clip(cols - NA_COLS // 2, 0, GRID_W - NA_COLS)
    col_in = (cols[None, :] >= col_start[:, None]) & (cols[None, :] < col_start[:, None] + NA_COLS)
    pad = GRID_W - NA_COLS
    padded = jnp.pad(rpb.astype(F32), ((0, 0), (0, 0), (pad, pad)), mode="edge")
    span = padded.shape[-1]
    tiled = jnp.tile(jnp.pad(padded, ((0, 0), (0, 0), (0, 1))), (1, 1, GRID_W))[..., :GRID_W * span]
    t = tiled.reshape(padded.shape[:2] + (GRID_W, span))[..., GRID_W - 1:]
    t = jnp.where(col_in[None, None], t * LOG2E, NEG_INF)
    t = jnp.swapaxes(t, 1, 2)
    bands = [t[:, :, NA_ROWS - 1 - d:NA_ROWS - 1 - d + kh].reshape(C_HEADS // 2, 2 * GRID_W, kh * GRID_W)
             for d in range(kh)]
    return jnp.stack(bands, axis=0)


def _na_attention(qkv, ckv, bias, rows):
    b, s, _ = qkv.shape
    c = ckv.shape[1]
    hd = C_HEADS * HEAD_DIM
    kh = min(NA_ROWS, rows)

    rps = next(k for k in (4, 2, 1) if rows % k == 0)

    def bias_spec(k):
        def bias_map(i, g):
            r = g * rps + k
            return (r - jnp.clip(r - kh // 2, 0, rows - kh), 0, 0, 0)
        return pl.BlockSpec((1,) + bias.shape[1:], bias_map)

    return pl.pallas_call(
        functools.partial(_na_kernel, rows=rows, kh=kh),
        out_shape=jax.ShapeDtypeStruct((b, s, hd), BF16),
        grid=(b, rows // rps),
        in_specs=[pl.BlockSpec((1, rps * GRID_W, hd), lambda i, g: (i, g, 0)),
                  pl.BlockSpec((1, s, hd), lambda i, g: (i, 0, 1)),
                  pl.BlockSpec((1, s, 2 * hd), lambda i, g: (i, 0, 1)),
                  pl.BlockSpec((1, c, hd), lambda i, g: (i, 0, 2)),
                  pl.BlockSpec((1, c, 2 * hd), lambda i, g: (i, 0, 0))] + [bias_spec(k) for k in range(rps)],
        out_specs=pl.BlockSpec((1, rps * GRID_W, hd), lambda i, g: (i, g, 0)),
        compiler_params=_params("parallel", "parallel"),
        name="na_attention",
    )(qkv, qkv, qkv, ckv, ckv, *([bias] * rps))


def _outproj_core(o_ref, w_ref, x_ref, m_ref, gpost_ref, gpre_ref):
    y = _dot(o_ref[0], w_ref[...])
    x = x_ref[0] + _rms_scaled(y, m_ref[0, 2:3, :] * gpost_ref[...])
    h = _rms_scaled(x, gpre_ref[...] * (1.0 + m_ref[0, 4:5, :])) + m_ref[0, 3:4, :]
    return x, h


def _outproj_kernel(o_ref, w_ref, x_ref, m_ref, gpost_ref, gpre_ref, xo_ref, h_ref):
    x, h = _outproj_core(o_ref, w_ref, x_ref, m_ref, gpost_ref, gpre_ref)
    xo_ref[0] = x
    h_ref[0] = h.astype(h_ref.dtype)


def _split_bf16(a):
    hi = a.astype(BF16)
    return hi, (a - hi.astype(F32)).astype(BF16)


def _outproj_router_kernel(o_ref, w_ref, x_ref, m_ref, gpost_ref, gpre_ref, wr_ref,
                           xo_ref, h_ref, rank_ref, gate_ref, pos_t_ref, cnt_ref):
    x, h = _outproj_core(o_ref, w_ref, x_ref, m_ref, gpost_ref, gpre_ref)
    xo_ref[0] = x
    h_ref[0] = h.astype(h_ref.dtype)
    h_hi, h_lo = _split_bf16(h)
    both = _dot(h_hi, wr_ref[...])
    logits = both[:, :LANES] + (both[:, LANES:] + _dot(h_lo, wr_ref[:, :LANES]))
    t = logits.shape[0]
    lane = lax.broadcasted_iota(jnp.int32, logits.shape, 1).astype(F32)
    logits = jnp.where(lane < N_EXPERTS, logits, -jnp.inf)
    v1 = jnp.max(logits, axis=-1, keepdims=True)
    i1 = jnp.min(jnp.where(logits == v1, lane, float(LANES)), axis=-1, keepdims=True)
    rest = jnp.where(lane == i1, -jnp.inf, logits)
    v2 = jnp.max(rest, axis=-1, keepdims=True)
    i2 = jnp.min(jnp.where(rest == v2, lane, float(LANES)), axis=-1, keepdims=True)
    e2 = jnp.exp(v2 - v1)
    w1 = 1.0 / (1.0 + e2)
    w2 = e2 / (1.0 + e2)
    sel1, sel2 = lane == i1, lane == i2
    sel = (sel1 | sel2).astype(F32)
    rr = lax.broadcasted_iota(jnp.int32, (t, t), 0)
    cc = lax.broadcasted_iota(jnp.int32, (t, t), 1)
    before = _dot(jnp.where(rr > cc, 1.0, 0.0).astype(BF16), sel.astype(BF16))
    rank = jnp.where(sel > 0.0, before, -1.0)
    rank_ref[0] = rank[:, :N_EXPERTS]
    gate_ref[0] = jnp.where(sel1, w1, jnp.where(sel2, w2, 0.0))[:, :N_EXPERTS]
    cnt = jnp.sum(sel, axis=0, keepdims=True)
    cnt_ref[0] = cnt
    chunk = jnp.floor((cnt + (ROW_ALIGN - 1)) * (1.0 / ROW_ALIGN)) * ROW_ALIGN
    ei = lax.broadcasted_iota(jnp.int32, (LANES, LANES), 0)
    ej = lax.broadcasted_iota(jnp.int32, (LANES, LANES), 1)
    offs = _dot(jnp.broadcast_to(chunk, (8, LANES)).astype(BF16), jnp.where(ei < ej, 1.0, 0.0).astype(BF16))[0:1]
    local = before + offs
    pos1 = jnp.sum(jnp.where(sel1, local, 0.0), axis=-1, keepdims=True)
    pos2 = jnp.sum(jnp.where(sel2, local, 0.0), axis=-1, keepdims=True)
    pos_t_ref[0] = jnp.where(lane == 0.0, pos1, jnp.where(lane == 1.0, pos2, 0.0)).T[:8, :]


def _outproj(o, w, x, mods, mod_off, g_post, g_pre, *, router=None, name):
    grp, n, d = x.shape
    dm = o.shape[2]
    tm = _tile(n, 512)
    tok = lambda b, t: (b, t, 0)
    const = lambda b, t: (0, 0)
    in_specs = [pl.BlockSpec((1, tm, dm), tok),
                pl.BlockSpec((dm, d), const),
                pl.BlockSpec((1, tm, d), tok),
                pl.BlockSpec((1, N_MOD, d), lambda b, t: (b + mod_off, 0, 0)),
                pl.BlockSpec((1, d), const),
                pl.BlockSpec((1, d), const)]
    args = [o, w, x, mods, g_post.reshape(1, d), g_pre.reshape(1, d)]
    if router is None:
        return pl.pallas_call(
            _outproj_kernel,
            out_shape=(jax.ShapeDtypeStruct((grp, n, d), F32), jax.ShapeDtypeStruct((grp, n, d), BF16)),
            grid=(grp, n // tm),
            in_specs=in_specs,
            out_specs=(pl.BlockSpec((1, tm, d), tok), pl.BlockSpec((1, tm, d), tok)),
            compiler_params=_params("parallel", "parallel"),
            name=name,
        )(*args)
    in_specs += [pl.BlockSpec((d, 2 * LANES), const)]
    args += [jnp.concatenate(router, axis=1)]
    per = n // tm
    blk = lambda b, t: (b * per + t, 0, 0)
    ne = N_EXPERTS
    return pl.pallas_call(
        _outproj_router_kernel,
        out_shape=(jax.ShapeDtypeStruct((grp, n, d), F32), jax.ShapeDtypeStruct((grp, n, d), BF16),
                   jax.ShapeDtypeStruct((grp, n, ne), F32), jax.ShapeDtypeStruct((grp, n, ne), F32),
                   jax.ShapeDtypeStruct((grp * per, 8, tm), F32), jax.ShapeDtypeStruct((grp * per, 1, LANES), F32)),
        grid=(grp, per),
        in_specs=in_specs,
        out_specs=(pl.BlockSpec((1, tm, d), tok), pl.BlockSpec((1, tm, d), tok),
                   pl.BlockSpec((1, tm, ne), tok), pl.BlockSpec((1, tm, ne), tok),
                   pl.BlockSpec((1, 8, tm), blk), pl.BlockSpec((1, 1, LANES), blk)),
        compiler_params=_params("parallel", "parallel"),
        name=name,
    )(*args)


def _swiglu_chunk(h, wg, wu, wd):
    g = _dot(h, wg)
    u = _dot(h, wu)
    a = (g / (1.0 + jnp.exp(-g))) * u
    return _dot(a.astype(BF16), wd)


SUB_ROWS = 256


def _accumulate_swiglu_rows(c, rows, x_ref, acc_ref, wg, wu, wd, out_ref):
    total = jnp.where(c > 0, acc_ref[rows], 0.0) + _swiglu_chunk(x_ref[rows], wg, wu, wd)
    acc_ref[rows] = total
    out_ref[rows] = total.astype(out_ref.dtype)


def _ffn_kernel(h_ref, wg_ref, wu_ref, wd_ref, x_ref, m_ref, g_ref, o_ref):
    n_rows = h_ref.shape[1]
    sub = min(SUB_ROWS, n_rows)
    for r in range(n_rows // sub):
        rows = slice(r * sub, (r + 1) * sub)
        y = _swiglu_chunk(h_ref[0, rows], wg_ref[...], wu_ref[...], wd_ref[...])
        o_ref[0, rows] = x_ref[0, rows] + _rms_scaled(y, m_ref[0, 5:6, :] * g_ref[...])


def _ffn(h, wg, wu, wd, x, mods, mod_off, g_post, *, name):
    grp, n, d = x.shape
    ff = wg.shape[1]
    tm = _tile(n, 1024)
    tok = lambda b, t: (b, t, 0)
    const = lambda b, t: (0, 0)
    resident = dict(index_map=const, pipeline_mode=pl.Buffered(1))
    return pl.pallas_call(
        _ffn_kernel,
        out_shape=jax.ShapeDtypeStruct((grp, n, d), F32),
        grid=(grp, n // tm),
        in_specs=[pl.BlockSpec((1, tm, d), tok),
                  pl.BlockSpec((d, ff), **resident),
                  pl.BlockSpec((d, ff), **resident),
                  pl.BlockSpec((ff, d), **resident),
                  pl.BlockSpec((1, tm, d), tok),
                  pl.BlockSpec((1, N_MOD, d), lambda b, t: (b + mod_off, 0, 0)),
                  pl.BlockSpec((1, d), const)],
        out_specs=pl.BlockSpec((1, tm, d), tok),
        compiler_params=_params("parallel", "parallel"),
        name=name,
    )(h, wg, wu, wd, x, mods, g_post.reshape(1, d))


ROW_ALIGN = 16


def _for_pieces(n, max_piece, fn):
    off = 0
    size = max_piece
    while size >= ROW_ALIGN:
        bit = (n & size) != 0

        @pl.when(bit)
        def _(off=off, size=size):
            fn(pl.multiple_of(off, ROW_ALIGN), size)

        off = off + jnp.where(bit, size, 0)
        size //= 2


def _dispatch_kernel(start_ref, n_ref, loc_ref, zstart_ref, zn_ref, h_ref, pos_t_ref, xs_ref, stage_ref, zero_ref,
                     sem, zsem):
    b = pl.program_id(0)
    last = pl.num_programs(0) - 1
    tm = h_ref.shape[0]
    h = h_ref[...]

    def out_copy(blk, e, off, size):
        start = start_ref[blk * N_EXPERTS + e]
        loc = loc_ref[blk * N_EXPERTS + e]
        src = stage_ref.at[blk % 2, pl.ds(pl.multiple_of(loc + off, ROW_ALIGN), size)]
        dst = xs_ref.at[pl.ds(pl.multiple_of(start + off, ROW_ALIGN), size)]
        return pltpu.make_async_copy(src, dst, sem.at[blk % 2, e])

    def for_chunk_copies(blk, op):
        for e in range(N_EXPERTS):
            _for_pieces(n_ref[blk * N_EXPERTS + e], tm, lambda off, size, e=e: op(out_copy(blk, e, off, size)))

    def zero_copy(start, off, size):
        dst = xs_ref.at[pl.ds(pl.multiple_of(start + off, ROW_ALIGN), size)]
        return pltpu.make_async_copy(zero_ref.at[pl.ds(0, size)], dst, zsem)

    @pl.when(b == 0)
    def _():
        zero_ref[...] = jnp.zeros_like(zero_ref)
        dead_from = zn_ref[N_EXPERTS]
        n_slabs = xs_ref.shape[0] // tm

        def slab(i):
            return zero_copy(pl.multiple_of(i * tm, tm), 0, tm)

        for k in range(N_EXPERTS):
            _for_pieces(zn_ref[k], tm, lambda off, size, k=k: zero_copy(zstart_ref[k], off, size).start())
        lax.fori_loop(dead_from, n_slabs, lambda i, carry: (slab(i).start(), carry)[1], 0)
        for k in range(N_EXPERTS):
            _for_pieces(zn_ref[k], tm, lambda off, size, k=k: zero_copy(zstart_ref[k], off, size).wait())
        lax.fori_loop(dead_from, n_slabs, lambda i, carry: (slab(i).wait(), carry)[1], 0)

    pos1, pos2 = pos_t_ref[0, 0:1, :], pos_t_ref[0, 1:2, :]
    n_stage = stage_ref.shape[1]
    for r0 in range(0, n_stage, SUB_ROWS):
        nr = min(SUB_ROWS, n_stage - r0)
        r = lax.broadcasted_iota(jnp.int32, (nr, tm), 0).astype(F32) + float(r0)
        onehot = jnp.where((r == pos1) | (r == pos2), 1.0, 0.0).astype(BF16)
        stage_ref[b % 2, r0:r0 + nr, :] = _dot(onehot, h).astype(stage_ref.dtype)
    for_chunk_copies(b, lambda cp: cp.start())

    @pl.when(b > 0)
    def _():
        for_chunk_copies(b - 1, lambda cp: cp.wait())

    @pl.when(b == last)
    def _():
        for_chunk_copies(b, lambda cp: cp.wait())


def _dispatch(h, pos_t, starts, counts, locs, zstarts, zcounts, n_rows):
    n, d = h.shape
    nb, _, tm = pos_t.shape
    n_stage = 2 * tm + -(-(ROW_ALIGN - 1) * N_EXPERTS // LANES) * LANES
    return pl.pallas_call(
        _dispatch_kernel,
        out_shape=jax.ShapeDtypeStruct((n_rows, d), BF16),
        grid_spec=pltpu.PrefetchScalarGridSpec(
            num_scalar_prefetch=5,
            grid=(nb,),
            in_specs=[pl.BlockSpec((tm, d), lambda i, *_: (i, 0)),
                      pl.BlockSpec((1, 8, tm), lambda i, *_: (i, 0, 0))],
            out_specs=pl.BlockSpec(memory_space=pl.ANY),
            scratch_shapes=[pltpu.VMEM((2, n_stage, d), BF16), pltpu.VMEM((tm, d), BF16),
                            pltpu.SemaphoreType.DMA((2, N_EXPERTS)), pltpu.SemaphoreType.DMA]),
        compiler_params=_params("arbitrary"),
        name="moe_dispatch",
    )(starts, counts, locs, zstarts, zcounts, h, pos_t)


def _experts_kernel(te_ref, nt_ref, used_ref, xs_ref, wg_ref, wu_ref, wd_ref, ys_ref, acc_ref):
    c = pl.program_id(1)
    used = used_ref[pl.program_id(0)]
    tmg = acc_ref.shape[0]
    sub = min(SUB_ROWS, tmg)
    subs = [slice(r * sub, (r + 1) * sub) for r in range(tmg // sub)]

    def run(rows):
        _accumulate_swiglu_rows(c, rows, xs_ref, acc_ref, wg_ref[0], wu_ref[0], wd_ref[0], ys_ref)

    @pl.when(used == tmg)
    def _():
        for rows in subs:
            run(rows)

    @pl.when(used < tmg)
    def _():
        for rows in subs:
            pl.when(used > rows.start)(functools.partial(run, rows))

            @pl.when((used <= rows.start) & (c == 0))
            def _(rows=rows):
                ys_ref[rows] = jnp.zeros((sub, ys_ref.shape[1]), ys_ref.dtype)


def _experts(xs, tile_expert, n_tiles, tile_used, wg, wu, wd, tmg):
    n_rows, d = xs.shape
    ff = wg.shape[2]
    fc = _tile(ff, 1792)
    total = n_rows // tmg

    def row_map(i, c, te, nt, used):
        return (jnp.minimum(i, nt[0] - 1), 0)

    def w_in_map(i, c, te, nt, used):
        return (te[jnp.minimum(i, nt[0] - 1)], 0, jnp.where(i < nt[0], c, ff // fc - 1))

    def w_out_map(i, c, te, nt, used):
        return (te[jnp.minimum(i, nt[0] - 1)], jnp.where(i < nt[0], c, ff // fc - 1), 0)

    return pl.pallas_call(
        _experts_kernel,
        out_shape=jax.ShapeDtypeStruct((n_rows, d), BF16),
        grid_spec=pltpu.PrefetchScalarGridSpec(
            num_scalar_prefetch=3,
            grid=(total, ff // fc),
            in_specs=[pl.BlockSpec((tmg, d), row_map),
                      pl.BlockSpec((1, d, fc), w_in_map),
                      pl.BlockSpec((1, d, fc), w_in_map),
                      pl.BlockSpec((1, fc, d), w_out_map)],
            out_specs=pl.BlockSpec((tmg, d), lambda i, c, te, nt, used: (i, 0)),
            scratch_shapes=[pltpu.VMEM((tmg, d), F32)]),
        compiler_params=_params("arbitrary", "arbitrary"),
        name="moe_experts",
    )(tile_expert, n_tiles, tile_used, xs, wg, wu, wd)


def _combine_kernel(start_ref, n_ref, ys_ref, rank_ref, gate_ref, x_ref, m_ref, g_ref, o_ref, buf_ref, acc_ref, sem):
    per = pl.num_programs(1)
    b = pl.program_id(0) * per + pl.program_id(1)
    tm = x_ref.shape[1]
    half = tm // 2

    n_blocks = pl.num_programs(0) * per

    def in_copy(blk, e, off, size):
        start = start_ref[blk * N_EXPERTS + e]
        src = ys_ref.at[pl.ds(pl.multiple_of(start + off, ROW_ALIGN), size)]
        return pltpu.make_async_copy(src, buf_ref.at[blk % 2, e, pl.ds(off, size)], sem.at[blk % 2, e])

    def fetch(blk):
        for e in range(N_EXPERTS):
            _for_pieces(n_ref[blk * N_EXPERTS + e], tm, lambda off, size, e=e: in_copy(blk, e, off, size).start())

    @pl.when(b == 0)
    def _():
        buf_ref[...] = jnp.zeros_like(buf_ref)
        fetch(b)

    @pl.when(b + 1 < n_blocks)
    def _():
        fetch(b + 1)

    for e in range(N_EXPERTS):
        _for_pieces(n_ref[b * N_EXPERTS + e], tm, lambda off, size, e=e: in_copy(b, e, off, size).wait())

    def gathered(e, hf):
        col = lax.broadcasted_iota(jnp.int32, (tm, half), 1).astype(F32) + float(hf * half)
        onehot = jnp.where(rank_ref[0, :, e:e + 1] == col, 1.0, 0.0).astype(BF16)
        return gate_ref[0, :, e:e + 1] * _dot(onehot, buf_ref[b % 2, e, hf * half:(hf + 1) * half, :])

    acc_ref[...] = functools.reduce(lambda a, c: a + c, [gathered(e, 0) for e in range(N_EXPERTS)])
    for e in range(N_EXPERTS):
        @pl.when(n_ref[b * N_EXPERTS + e] > half)
        def _(e=e):
            acc_ref[...] += gathered(e, 1)
    o_ref[0] = x_ref[0] + _rms_scaled(acc_ref[...], m_ref[0, 5:6, :] * g_ref[...])


def _combine(ys, starts, counts, rank, gate, x, mods, g_post, tm):
    b, s, d = x.shape
    per = s // tm
    tok = lambda i, t, *_: (i, t, 0)
    return pl.pallas_call(
        _combine_kernel,
        out_shape=jax.ShapeDtypeStruct((b, s, d), F32),
        grid_spec=pltpu.PrefetchScalarGridSpec(
            num_scalar_prefetch=2,
            grid=(b, per),
            in_specs=[pl.BlockSpec(memory_space=pl.ANY),
                      pl.BlockSpec((1, tm, N_EXPERTS), tok),
                      pl.BlockSpec((1, tm, N_EXPERTS), tok),
                      pl.BlockSpec((1, tm, d), tok),
                      pl.BlockSpec((1, N_MOD, d), lambda i, t, *_: (i, 0, 0)),
                      pl.BlockSpec((1, d), lambda i, t, *_: (0, 0))],
            out_specs=pl.BlockSpec((1, tm, d), tok),
            scratch_shapes=[pltpu.VMEM((2, N_EXPERTS, tm, d), BF16), pltpu.VMEM((tm, d), F32),
                            pltpu.SemaphoreType.DMA((2, N_EXPERTS))]),
        compiler_params=_params("arbitrary", "arbitrary"),
        name="moe_combine",
    )(starts, counts, ys, rank, gate, x, mods, g_post.reshape(1, d))


def _moe(h, rank, gate, pos_t, cnt, x, mods, g_post, wg, wu, wd):
    b, s, d = x.shape
    n = b * s
    nb, _, tm = pos_t.shape
    tmg = _tile(2 * n, 1024)
    total_tiles = -(-(2 * n + (ROW_ALIGN - 1) * N_EXPERTS * nb) // tmg) + N_EXPERTS
    n_rows = total_tiles * tmg
    per_block = cnt[:, 0, :N_EXPERTS].astype(jnp.int32)
    chunk = (per_block + ROW_ALIGN - 1) // ROW_ALIGN * ROW_ALIGN
    rows_e = jnp.sum(chunk, axis=0)
    tiles_per = (rows_e + tmg - 1) // tmg
    tile_end = jnp.cumsum(tiles_per)
    offsets = (tile_end - tiles_per) * tmg
    starts = (offsets[None, :] + jnp.cumsum(chunk, axis=0) - chunk).reshape(nb * N_EXPERTS)
    counts = chunk.reshape(nb * N_EXPERTS)
    locs = (jnp.cumsum(chunk, axis=1) - chunk).reshape(nb * N_EXPERTS)
    zstarts = offsets + rows_e
    zcounts = jnp.concatenate([tiles_per * tmg - rows_e, tile_end[-1:] * (tmg // tm)])
    tile_expert = jnp.sum((tile_end[None, :] <= jnp.arange(total_tiles)[:, None]).astype(jnp.int32), axis=1)
    tile_expert = jnp.minimum(tile_expert, N_EXPERTS - 1)
    n_tiles = tile_end[-1:].astype(jnp.int32)
    tile_index = jnp.arange(total_tiles) - (tile_end - tiles_per)[tile_expert]
    tile_used = jnp.clip(rows_e[tile_expert] - tile_index * tmg, 0, tmg).astype(jnp.int32)
    xs = _dispatch(h.reshape(n, d), pos_t, starts, counts, locs, zstarts, zcounts.astype(jnp.int32), n_rows)
    ys = _experts(xs, tile_expert, n_tiles, tile_used, wg, wu, wd, tmg)
    return _combine(ys, starts, counts, rank, gate, x, mods, g_post, tm)


def _even_layouts(w_in, w_out, g_q, g_k, sink):
    dh = HEAD_DIM
    within = np.concatenate([np.arange(0, dh, 2), np.arange(1, dh, 2)])
    g = A_Q_HEADS // A_KV_HEADS
    q_order = np.array([h for p in range(g) for h in _pair_heads(p)])
    sizes = [A_Q_HEADS * dh, A_KV_HEADS * dh, A_KV_HEADS * dh, B_Q_HEADS * dh, B_KV_HEADS * dh, B_KV_HEADS * dh]
    base = np.concatenate([[0], np.cumsum(sizes)])

    def head_cols(start, order, permute):
        inner = within if permute else np.arange(dh)
        return np.concatenate([start + h * dh + inner for h in order])

    kv_order = np.arange(A_KV_HEADS)
    cols = np.concatenate([head_cols(base[0], q_order, True), head_cols(base[3], q_order, True),
                           head_cols(base[1], kv_order, True), head_cols(base[4], kv_order, True),
                           head_cols(base[2], kv_order, False), head_cols(base[5], kv_order, False)])
    rows_out = np.concatenate([head_cols(0, q_order, False), head_cols(A_Q_HEADS * dh, q_order, False)])
    gq = jnp.tile(g_q[within], LANES // dh).reshape(1, LANES)
    gk = jnp.tile(g_k[within], LANES // dh).reshape(1, LANES)
    lane = np.arange(LANES)
    head_mean = jnp.asarray((lane[:, None] // dh == lane[None, :] // dh) / dh, BF16)
    return (w_in[:, cols].astype(BF16), w_out[rows_out].astype(BF16), gq, gk, head_mean,
            sink.astype(F32) * LOG2E)


def _rope_tables(n_tokens):
    t = jnp.arange(n_tokens, dtype=jnp.int32)
    row = (t // GRID_W).astype(F32)
    col = (t % GRID_W).astype(F32)
    half = HEAD_DIM // 2
    freqs = ROPE_THETA ** (-jnp.arange(0, half, 2, dtype=F32) / half)
    ang = jnp.concatenate([row[:, None] * freqs, col[:, None] * freqs], axis=-1)
    cos, sin = jnp.cos(ang), jnp.sin(ang)
    reps = LANES // HEAD_DIM
    return (jnp.tile(jnp.concatenate([cos, cos], axis=-1), (1, reps)),
            jnp.tile(jnp.concatenate([-sin, sin], axis=-1), (1, reps)))


def kernel(x, c, ctx, c_ctx, e_w_mod, e_b_mod, e_g_pre_mix, e_g_post_mix, e_g_pre_ffn, e_g_post_ffn, e_w_in, e_w_out, e_g_q, e_g_k, e_sink, e_w_gate, e_w_up, e_w_down, o_w_mod, o_b_mod, o_g_pre_mix, o_g_post_mix, o_g_pre_ffn, o_g_post_ffn, o_w_in, o_w_out, o_rpb, o_w_router, o_w_gate, o_w_up, o_w_down):
    b, s, d = x.shape
    n_ctx = ctx.shape[1]
    rows = s // GRID_W
    ctx = ctx.reshape(1, b * n_ctx, d)

    mods = _mod_vectors(c, c_ctx, e_w_mod[0], e_b_mod[0])
    w_in, w_out, gq, gk, head_mean, sink = _even_layouts(e_w_in[0], e_w_out[0], e_g_q[0], e_g_k[0], e_sink[0])
    even = (gq, gk, head_mean)
    qkv = _inproj(x, mods, 0, e_g_pre_mix[0], w_in, even=even, rope_tabs=_rope_tables(s), name="inproj0_lat")
    ckv = _inproj(ctx, mods, b, e_g_pre_mix[0], w_in, even=even, rope_tabs=(None, None), name="inproj0_ctx")
    ckv = ckv.reshape(b, n_ctx, -1)
    o_lat = _even_attention(qkv, ckv, sink)
    o_ctx = _even_ctx_attention(ckv, sink).reshape(1, b * n_ctx, -1)
    wg, wu, wd = e_w_gate[0].astype(BF16), e_w_up[0].astype(BF16), e_w_down[0].astype(BF16)
    x, h = _outproj(o_lat, w_out, x, mods, 0, e_g_post_mix[0], e_g_pre_ffn[0], name="outproj0_lat")
    x = _ffn(h, wg, wu, wd, x, mods, 0, e_g_post_ffn[0], name="ffn0_lat")
    ctx, h = _outproj(o_ctx, w_out, ctx, mods, b, e_g_post_mix[0], e_g_pre_ffn[0], name="outproj0_ctx")
    ctx = _ffn(h, wg, wu, wd, ctx, mods, b, e_g_post_ffn[0], name="ffn0_ctx")

    mods = _mod_vectors(c, c_ctx, o_w_mod[0], o_b_mod[0])
    hd = C_HEADS * HEAD_DIM
    w_in = o_w_in[0].astype(BF16)
    qkv = _inproj(x, mods, 0, o_g_pre_mix[0], w_in, segments=(("q", hd), ("k", hd), ("v", hd)), name="inproj1_lat")
    w_vk = jnp.concatenate([w_in[:, 2 * hd:], w_in[:, hd:2 * hd]], axis=1)
    ckv = _inproj(ctx, mods, b, o_g_pre_mix[0], w_vk, segments=(("v", hd), ("k", hd)), name="inproj1_ctx")
    ckv = ckv.reshape(b, n_ctx, 3 * hd)
    o_lat = _na_attention(qkv, ckv, _na_bias_table(o_rpb[0], rows), rows)
    wr = jnp.zeros((d, LANES), F32).at[:, :N_EXPERTS].set(o_w_router[0])
    x, h, rank, gate, pos_t, cnt = _outproj(o_lat, o_w_out[0].astype(BF16), x, mods, 0, o_g_post_mix[0],
                                            o_g_pre_ffn[0], router=_split_bf16(wr), name="outproj1_router")
    return _moe(h, rank, gate, pos_t, cnt, x, mods, o_g_post_ffn[0],
                o_w_gate[0].astype(BF16), o_w_up[0].astype(BF16), o_w_down[0].astype(BF16))
```
